```python
import math
import jax, jax.numpy as jnp
from jax import lax
import numpy as np

D_MODEL = 1024
BATCH = 2
SEQ = 16384
DEPTH = 2

N_Q = 8
N_KV = 2
GROUP = N_Q // N_KV
HEAD_DIM = 64
ATTN_W = N_Q * HEAD_DIM
KV_W = N_KV * HEAD_DIM
WINDOW = 128
BLOCK = 128
CONV_C = D_MODEL // 2
CONV_K = 31
D_FF = 4 * D_MODEL
IN_W = ATTN_W + 2 * KV_W + 2 * CONV_C + 2 * D_MODEL
EPS = 1e-6
NEG = -1e30

kernel_name = "hybrid_swa_sink_alibi_conformer_conv_gated_block"


def rms_norm(x, g):
    xf = x.astype(jnp.float32)
    y = xf * lax.rsqrt(jnp.mean(xf * xf, axis=-1, keepdims=True) + EPS)
    return (y * g.astype(jnp.float32)).astype(x.dtype)


def layer_norm(x, g, b):
    xf = x.astype(jnp.float32)
    mu = jnp.mean(xf, axis=-1, keepdims=True)
    xc = xf - mu
    var = jnp.mean(xc * xc, axis=-1, keepdims=True)
    y = xc * lax.rsqrt(var + EPS) * g.astype(jnp.float32) + b.astype(jnp.float32)
    return y.astype(x.dtype)


def alibi_slopes():
    return jnp.asarray(2.0 ** (-8.0 * np.arange(1, N_Q + 1) / N_Q), dtype=jnp.float32)


def sliding_window_attention(q, k, v, sinks):
    B, S = q.shape[0], q.shape[1]
    nb = S // BLOCK
    qb = q.reshape(B, nb, BLOCK, N_KV, GROUP, HEAD_DIM)

    def band(t):
        t = t.reshape(B, S, N_KV, HEAD_DIM)
        tp = jnp.pad(t, ((0, 0), (BLOCK, 0), (0, 0), (0, 0))).reshape(B, nb + 1, BLOCK, N_KV, HEAD_DIM)
        return jnp.concatenate([tp[:, :-1], tp[:, 1:]], axis=2)

    kb, vb = band(k), band(v)
    scale = 1.0 / math.sqrt(HEAD_DIM)
    s = jnp.einsum('bnqkgd,bnskd->bkgnqs', qb, kb).astype(jnp.float32) * scale

    qi = jnp.arange(BLOCK)[:, None] + BLOCK
    kj = jnp.arange(2 * BLOCK)[None, :]
    dist = qi - kj
    key_pos = jnp.arange(nb)[:, None, None] * BLOCK + kj[None] - BLOCK
    valid = (dist >= 0)[None] & (dist < WINDOW)[None] & (key_pos >= 0)

    slopes = alibi_slopes().reshape(N_KV, GROUP)
    s = s - slopes[None, :, :, None, None, None] * dist.astype(jnp.float32)
    s = jnp.where(valid, s, NEG)

    sink = sinks.astype(jnp.float32).reshape(N_KV, GROUP)[None, :, :, None, None]
    m = jnp.maximum(jnp.max(s, axis=-1), sink)
    p = jnp.exp(s - m[..., None])
    denom = jnp.sum(p, axis=-1) + jnp.exp(sink - m)
    p = p / denom[..., None]
    o = jnp.einsum('bkgnqs,bnskd->bnqkgd', p.astype(v.dtype), vb)
    return o.reshape(B, S, ATTN_W)


def causal_depthwise_conv(u, w, b):
    y = lax.conv_general_dilated(
        u, w[:, None, :].astype(u.dtype), window_strides=(1,), padding=[(CONV_K - 1, 0)],
        dimension_numbers=('NWC', 'WIO', 'NWC'), feature_group_count=CONV_C)
    return y + b


def setup_inputs(seed: int = 0) -> dict:
    key = jax.random.key(seed)
    ks = jax.random.split(key, 20)
    f = jnp.float32
    nrm = lambda k, shp, s: jax.random.normal(k, shp, f) * s
    return {
        "x": nrm(ks[0], (BATCH, SEQ, D_MODEL), 1.0),
        "mix_norm_g": 1.0 + nrm(ks[1], (DEPTH, D_MODEL), 0.01),
        "w_in": nrm(ks[2], (DEPTH, D_MODEL, IN_W), D_MODEL ** -0.5),
        "b_in": nrm(ks[3], (DEPTH, IN_W), 0.02),
        "sinks": nrm(ks[4], (DEPTH, N_Q), 0.5),
        "conv_w": nrm(ks[5], (DEPTH, CONV_K, CONV_C), CONV_K ** -0.5),
        "conv_b": nrm(ks[6], (DEPTH, CONV_C), 0.02),
        "conv_ln_g": 1.0 + nrm(ks[7], (DEPTH, CONV_C), 0.01),
        "conv_ln_b": nrm(ks[8], (DEPTH, CONV_C), 0.01),
        "w_attn_proj": nrm(ks[9], (DEPTH, ATTN_W, D_MODEL), ATTN_W ** -0.5),
        "w_conv_proj": nrm(ks[10], (DEPTH, CONV_C, D_MODEL), CONV_C ** -0.5),
        "b_conv_proj": nrm(ks[11], (DEPTH, D_MODEL), 0.02),
        "w_out": nrm(ks[12], (DEPTH, D_MODEL, D_MODEL), D_MODEL ** -0.5),
        "mlp_norm_g": 1.0 + nrm(ks[13], (DEPTH, D_MODEL), 0.01),
        "w_mlp1": nrm(ks[14], (DEPTH, D_MODEL, D_FF), D_MODEL ** -0.5),
        "w_mlp2": nrm(ks[15], (DEPTH, D_FF, D_MODEL), D_FF ** -0.5),
        "final_norm_g": 1.0 + nrm(ks[16], (D_MODEL,), 0.01),
    }


def reference(x, mix_norm_g, w_in, b_in, sinks, conv_w, conv_b, conv_ln_g, conv_ln_b,
              w_attn_proj, w_conv_proj, b_conv_proj, w_out, mlp_norm_g, w_mlp1, w_mlp2,
              final_norm_g):
    splits = np.cumsum([ATTN_W, KV_W, KV_W, CONV_C, CONV_C, D_MODEL]).tolist()
    for l in range(DEPTH):
        h = rms_norm(x, mix_norm_g[l])
        proj = jnp.einsum('bsd,de->bse', h, w_in[l]) + b_in[l]
        q, k, v, glu_a, glu_b, gate_a, gate_c = jnp.split(proj, splits, axis=-1)

        attn = sliding_window_attention(q, k, v, sinks[l])
        br_a = jnp.einsum('bse,ed->bsd', attn, w_attn_proj[l])

        u = glu_a * jax.nn.sigmoid(glu_b)
        u = causal_depthwise_conv(u, conv_w[l], conv_b[l])
        u = jax.nn.silu(layer_norm(u, conv_ln_g[l], conv_ln_b[l]))
        br_c = jnp.einsum('bsc,cd->bsd', u, w_conv_proj[l]) + b_conv_proj[l]

        merged = jax.nn.sigmoid(gate_a) * br_a + jax.nn.sigmoid(gate_c) * br_c
        x = x + jnp.einsum('bsd,de->bse', merged, w_out[l])

        h2 = rms_norm(x, mlp_norm_g[l])
        a = jnp.square(jax.nn.relu(jnp.einsum('bsd,df->bsf', h2, w_mlp1[l])))
        x = x + jnp.einsum('bsf,fd->bsd', a, w_mlp2[l])
    return rms_norm(x, final_norm_g)
```

```python
import functools
import math

import jax
import jax.numpy as jnp
from jax import lax
from jax.experimental import pallas as pl
from jax.experimental.pallas import tpu as pltpu

D_MODEL = 1024
N_Q = 8
N_KV = 2
GROUP = N_Q // N_KV
HEAD_DIM = 64
ATTN_W = N_Q * HEAD_DIM
KV_W = N_KV * HEAD_DIM
WINDOW = 128
BLOCK = 128
CONV_C = D_MODEL // 2
CONV_K = 31
D_FF = 4 * D_MODEL
IN_W = ATTN_W + 2 * KV_W + 2 * CONV_C + 2 * D_MODEL
EPS = 1e-6
NEG = -1e30
SCALE = 1.0 / math.sqrt(HEAD_DIM)
SLOPES = tuple(2.0 ** (-8.0 * h / N_Q) for h in range(1, N_Q + 1))

Q0 = 0
K0 = Q0 + ATTN_W
GA0 = K0 + 2 * KV_W
GB0 = GA0 + CONV_C
GTA0 = GB0 + CONV_C
GTC0 = GTA0 + D_MODEL

LANES = 128
GW = GROUP * HEAD_DIM
UPAD = 32
TQ = 512
TM = 512
CONV_ROWS = 64
FF_CHUNK = 512
VMEM_LIMIT = 56 * 1024 * 1024

F32 = jnp.float32
BF16 = jnp.bfloat16


def _dot(a, b):
    return jnp.dot(a, b, preferred_element_type=F32)


def _sigmoid(x):
    return 1.0 / (1.0 + jnp.exp(-x))


def _mixer_kernel(sinks_ref, x_ref, g_ref, win_ref, bin_ref, cw_ref, cb_ref, lng_ref, lnb_ref,
                  wap_ref, wcp_ref, bcp_ref, wout_ref, o_ref,
                  h_s, q_s, kt_s, vt_s, u_s, y_s, attn_s, cv_s, mg_s):
    j = pl.program_id(1)
    tq = x_ref.shape[0]

    @pl.when(j == 0)
    def _():
        kt_s[:, 0:BLOCK, :] = jnp.zeros((N_KV, BLOCK, GW), BF16)
        vt_s[:, 0:BLOCK, :] = jnp.zeros((N_KV, BLOCK, GW), BF16)
        u_s[0:UPAD, :] = jnp.zeros((UPAD, CONV_C), F32)

    x = x_ref[...]
    ms = jnp.mean(x * x, axis=-1, keepdims=True)
    h_s[...] = (x * lax.rsqrt(ms + EPS) * g_ref[...]).astype(BF16)

    def proj(c0, width):
        return _dot(h_s[...], win_ref[:, c0:c0 + width]) + bin_ref[:, c0:c0 + width]

    q_s[...] = (proj(Q0, ATTN_W) * SCALE).astype(BF16)

    kv = proj(K0, 2 * KV_W)
    lo = lax.broadcasted_iota(jnp.int32, (tq, LANES), 1) < HEAD_DIM
    for src, dst in ((kv[:, :KV_W], kt_s), (kv[:, KV_W:], vt_s)):
        rolled = pltpu.roll(src, HEAD_DIM, axis=1)
        g0 = jnp.where(lo, src, rolled).astype(BF16)
        g1 = jnp.where(lo, rolled, src).astype(BF16)
        dst[0, BLOCK:BLOCK + tq, :] = jnp.concatenate([g0, g0], axis=1)
        dst[1, BLOCK:BLOCK + tq, :] = jnp.concatenate([g1, g1], axis=1)

    u_s[UPAD:UPAD + tq, :] = proj(GA0, CONV_C) * _sigmoid(proj(GB0, CONV_C))

    qi = lax.broadcasted_iota(jnp.int32, (BLOCK, 2 * BLOCK), 0)
    kj = lax.broadcasted_iota(jnp.int32, (BLOCK, 2 * BLOCK), 1)
    dist = qi + BLOCK - kj
    in_window = (dist >= 0) & (dist < WINDOW)
    distf = dist.astype(F32)
    head_of_lane = lax.broadcasted_iota(jnp.int32, (BLOCK, GW), 1) // HEAD_DIM
    kmin_first = jnp.where(j == 0, BLOCK, 0)
    for qb in range(tq // BLOCK):
        valid = in_window & (kj >= kmin_first) if qb == 0 else in_window
        rows = slice(qb * BLOCK, (qb + 1) * BLOCK)
        for g in range(N_KV):
            q4 = q_s[rows, g * GW:(g + 1) * GW]
            lhs = jnp.concatenate(
                [jnp.where(head_of_lane == hh, q4, jnp.zeros_like(q4)) for hh in range(GROUP)], axis=0)
            keys = kt_s[g, qb * BLOCK:qb * BLOCK + 2 * BLOCK, :]
            s = lax.dot_general(lhs, keys, (((1,), (1,)), ((), ())), preferred_element_type=F32)
            ps, inv_ls = [], []
            for hh in range(GROUP):
                head = g * GROUP + hh
                sink = sinks_ref[head]
                sh = s[hh * BLOCK:(hh + 1) * BLOCK, :] - SLOPES[head] * distf
                sh = jnp.where(valid, sh, NEG)
                m = jnp.maximum(jnp.max(sh, axis=-1, keepdims=True), sink)
                p = jnp.exp(sh - m)
                denom = jnp.sum(p, axis=-1, keepdims=True) + jnp.exp(sink - m)
                ps.append(p.astype(BF16))
                inv_ls.append(1.0 / denom)
            pmat = jnp.concatenate(ps, axis=0)
            o = _dot(pmat, vt_s[g, qb * BLOCK:qb * BLOCK + 2 * BLOCK, :])
            out = jnp.zeros((BLOCK, GW), F32)
            for hh in range(GROUP):
                out = jnp.where(head_of_lane == hh, o[hh * BLOCK:(hh + 1) * BLOCK, :] * inv_ls[hh], out)
            attn_s[rows, g * GW:(g + 1) * GW] = out.astype(BF16)

    for r in range(tq // CONV_ROWS):
        for c in range(CONV_C // LANES):
            lanes = slice(c * LANES, (c + 1) * LANES)
            acc = jnp.broadcast_to(cb_ref[:, lanes], (CONV_ROWS, LANES))
            for t in range(CONV_K):
                r0 = r * CONV_ROWS + UPAD - (CONV_K - 1) + t
                acc = acc + cw_ref[t:t + 1, lanes] * u_s[r0:r0 + CONV_ROWS, lanes]
            y_s[r * CONV_ROWS:(r + 1) * CONV_ROWS, lanes] = acc
    y = y_s[...]
    mu = jnp.mean(y, axis=-1, keepdims=True)
    yc = y - mu
    var = jnp.mean(yc * yc, axis=-1, keepdims=True)
    z = yc * lax.rsqrt(var + EPS) * lng_ref[...] + lnb_ref[...]
    cv_s[...] = (z * _sigmoid(z)).astype(BF16)

    half = D_MODEL // 2
    for nc in range(2):
        cols = slice(nc * half, (nc + 1) * half)
        br_a = _dot(attn_s[...], wap_ref[:, cols])
        br_c = _dot(cv_s[...], wcp_ref[:, cols]) + bcp_ref[:, cols]
        gate_a = _sigmoid(proj(GTA0 + nc * half, half))
        gate_c = _sigmoid(proj(GTC0 + nc * half, half))
        mg_s[:, cols] = (gate_a * br_a + gate_c * br_c).astype(BF16)
    o_ref[...] = x_ref[...] + _dot(mg_s[...], wout_ref[...])

    kt_s[:, 0:BLOCK, :] = kt_s[:, tq:tq + BLOCK, :]
    vt_s[:, 0:BLOCK, :] = vt_s[:, tq:tq + BLOCK, :]
    u_s[0:UPAD, :] = u_s[tq:tq + UPAD, :]


def _mlp_kernel(x_ref, g_ref, w1_ref, w2_ref, gf_ref, o_ref, h_s, a_s, *, final_norm):
    x = x_ref[...]
    ms = jnp.mean(x * x, axis=-1, keepdims=True)
    h_s[...] = (x * lax.rsqrt(ms + EPS) * g_ref[...]).astype(BF16)
    for c in range(D_FF // FF_CHUNK):
        cols = slice(c * FF_CHUNK, (c + 1) * FF_CHUNK)
        a = jnp.maximum(_dot(h_s[...], w1_ref[:, cols]), 0.0)
        a_s[:, cols] = (a * a).astype(BF16)
    y = x_ref[...] + _dot(a_s[...], w2_ref[...])
    if final_norm:
        ms2 = jnp.mean(y * y, axis=-1, keepdims=True)
        y = y * lax.rsqrt(ms2 + EPS) * gf_ref[...]
    o_ref[...] = y


def _resident(shape):
    return pl.BlockSpec(shape, lambda *_: (0,) * len(shape), pipeline_mode=pl.Buffered(1))


def _mixer(x, sinks, g, w_in, b_in, cw, cb, lng, lnb, wap, wcp, bcp, wout):
    b, s, d = x.shape
    tile = pl.BlockSpec((None, TQ, d), lambda bi, ji: (bi, ji, 0))
    return pl.pallas_call(
        _mixer_kernel,
        grid=(b, s // TQ),
        in_specs=[
            pl.BlockSpec(memory_space=pltpu.SMEM),
            tile,
            _resident((1, d)),
            _resident((d, IN_W)),
            _resident((1, IN_W)),
            _resident((CONV_K, CONV_C)),
            _resident((1, CONV_C)),
            _resident((1, CONV_C)),
            _resident((1, CONV_C)),
            _resident((ATTN_W, d)),
            _resident((CONV_C, d)),
            _resident((1, d)),
            _resident((d, d)),
        ],
        out_specs=tile,
        out_shape=jax.ShapeDtypeStruct(x.shape, x.dtype),
        scratch_shapes=[
            pltpu.VMEM((TQ, d), BF16),
            pltpu.VMEM((TQ, ATTN_W), BF16),
            pltpu.VMEM((N_KV, BLOCK + TQ, GW), BF16),
            pltpu.VMEM((N_KV, BLOCK + TQ, GW), BF16),
            pltpu.VMEM((UPAD + TQ, CONV_C), F32),
            pltpu.VMEM((TQ, CONV_C), F32),
            pltpu.VMEM((TQ, ATTN_W), BF16),
            pltpu.VMEM((TQ, CONV_C), BF16),
            pltpu.VMEM((TQ, d), BF16),
        ],
        compiler_params=pltpu.CompilerParams(
            dimension_semantics=("arbitrary", "arbitrary"), vmem_limit_bytes=VMEM_LIMIT),
        name="mixer",
    )(sinks, x, g, w_in, b_in, cw, cb, lng, lnb, wap, wcp, bcp, wout)


def _mlp(x2, g, w1, w2, gf, final_norm):
    t, d = x2.shape
    tile = pl.BlockSpec((TM, d), lambda i: (i, 0))
    return pl.pallas_call(
        functools.partial(_mlp_kernel, final_norm=final_norm),
        grid=(t // TM,),
        in_specs=[tile, _resident((1, d)), _resident((d, D_FF)), _resident((D_FF, d)), _resident((1, d))],
        out_specs=tile,
        out_shape=jax.ShapeDtypeStruct(x2.shape, x2.dtype),
        scratch_shapes=[pltpu.VMEM((TM, d), BF16), pltpu.VMEM((TM, D_FF), BF16)],
        compiler_params=pltpu.CompilerParams(
            dimension_semantics=("arbitrary",), vmem_limit_bytes=VMEM_LIMIT),
        name="mlp",
    )(x2, g, w1, w2, gf)


def kernel(x, mix_norm_g, w_in, b_in, sinks, conv_w, conv_b, conv_ln_g, conv_ln_b, w_attn_proj, w_conv_proj,
           b_conv_proj, w_out, mlp_norm_g, w_mlp1, w_mlp2, final_norm_g):
    b, s, d = x.shape
    depth = w_in.shape[0]
    assert s % TQ == 0 and (b * s) % TM == 0 and TQ % BLOCK == 0 and TQ % CONV_ROWS == 0
    row = lambda v: v.reshape(1, -1)
    for l in range(depth):
        x = _mixer(x, sinks[l], row(mix_norm_g[l]), w_in[l].astype(BF16), row(b_in[l]), conv_w[l],
                   row(conv_b[l]), row(conv_ln_g[l]), row(conv_ln_b[l]), w_attn_proj[l].astype(BF16),
                   w_conv_proj[l].astype(BF16), row(b_conv_proj[l]), w_out[l].astype(BF16))
        x = _mlp(x.reshape(b * s, d), row(mlp_norm_g[l]), w_mlp1[l].astype(BF16), w_mlp2[l].astype(BF16),
                 row(final_norm_g), final_norm=(l == depth - 1)).reshape(b, s, d)
    return x
```

```python
import functools
import math

import jax
import jax.numpy as jnp
from jax import lax
from jax.experimental import pallas as pl
from jax.experimental.pallas import tpu as pltpu

D_MODEL = 1024
N_Q = 8
N_KV = 2
GROUP = N_Q // N_KV
HEAD_DIM = 64
ATTN_W = N_Q * HEAD_DIM
KV_W = N_KV * HEAD_DIM
WINDOW = 128
BLOCK = 128
CONV_C = D_MODEL // 2
CONV_K = 31
D_FF = 4 * D_MODEL
IN_W = ATTN_W + 2 * KV_W + 2 * CONV_C + 2 * D_MODEL
EPS = 1e-6
NEG = -1e30
SCALE = 1.0 / math.sqrt(HEAD_DIM)
SLOPES = tuple(2.0 ** (-8.0 * h / N_Q) for h in range(1, N_Q + 1))

Q0 = 0
K0 = Q0 + ATTN_W
GA0 = K0 + 2 * KV_W
GB0 = GA0 + CONV_C
GTA0 = GB0 + CONV_C
GTC0 = GTA0 + D_MODEL

LANES = 128
GW = GROUP * HEAD_DIM
N_SLAB = CONV_C // LANES
UPAD = 32
TQ = 512
TM = 512
CONV_ROWS = 64
FF_CHUNK = 512
VMEM_LIMIT = 56 * 1024 * 1024

F32 = jnp.float32
BF16 = jnp.bfloat16


def _dot(a, b):
    return jnp.dot(a, b, preferred_element_type=F32)


def _sigmoid(x):
    return 0.5 * jnp.tanh(0.5 * x) + 0.5


def _mixer_kernel(sinks_ref, x_ref, g_ref, win_ref, bin_ref, cw_ref, cb_ref, lng_ref, lnb_ref,
                  wap_ref, wcp_ref, bcp_ref, wout_ref, o_ref,
                  h_s, q_s, kt_s, vt_s, u_s, y_s, attn_s, cv_s, mg_s, alibi_s):
    j = pl.program_id(1)
    tq = x_ref.shape[0]

    qi = lax.broadcasted_iota(jnp.int32, (BLOCK, 2 * BLOCK), 0)
    kj = lax.broadcasted_iota(jnp.int32, (BLOCK, 2 * BLOCK), 1)
    dist = qi + BLOCK - kj
    in_window = (dist >= 0) & (dist < WINDOW)

    @pl.when(j == 0)
    def _():
        kt_s[:, 0:BLOCK, :] = jnp.zeros((N_KV, BLOCK, GW), BF16)
        vt_s[:, 0:BLOCK, :] = jnp.zeros((N_KV, BLOCK, GW), BF16)
        u_s[:, 0:UPAD, :] = jnp.zeros((N_SLAB, UPAD, LANES), F32)
        distf = dist.astype(F32)
        for head in range(N_Q):
            alibi_s[head] = SLOPES[head] * distf

    x = x_ref[...]
    ms = jnp.mean(x * x, axis=-1, keepdims=True)
    h_s[...] = (x * lax.rsqrt(ms + EPS) * g_ref[...]).astype(BF16)

    def proj(c0, width):
        return _dot(h_s[...], win_ref[:, c0:c0 + width]) + bin_ref[:, c0:c0 + width]

    u = proj(GA0, CONV_C) * _sigmoid(proj(GB0, CONV_C))
    for c in range(N_SLAB):
        u_s[c, UPAD:UPAD + tq, :] = u[:, c * LANES:(c + 1) * LANES]

    q_s[...] = proj(Q0, ATTN_W).astype(BF16)

    kv = proj(K0, 2 * KV_W)
    lo = lax.broadcasted_iota(jnp.int32, (tq, LANES), 1) < HEAD_DIM
    for src, dst in ((kv[:, :KV_W] * SCALE, kt_s), (kv[:, KV_W:], vt_s)):
        rolled = pltpu.roll(src, HEAD_DIM, axis=1)
        g0 = jnp.where(lo, src, rolled).astype(BF16)
        g1 = jnp.where(lo, rolled, src).astype(BF16)
        dst[0, BLOCK:BLOCK + tq, :] = jnp.concatenate([g0, g0], axis=1)
        dst[1, BLOCK:BLOCK + tq, :] = jnp.concatenate([g1, g1], axis=1)

    for r in range(tq // CONV_ROWS):
        for c in range(N_SLAB):
            base = r * CONV_ROWS + UPAD - (CONV_K - 1)
            taps = [u_s[c, pl.ds(base + t, CONV_ROWS // 2, stride=2), :] for t in range(CONV_K + 1)]
            bias = jnp.broadcast_to(cb_ref[:, c * LANES:(c + 1) * LANES], (CONV_ROWS // 2, LANES))
            even, odd = bias, bias
            for t in range(CONV_K):
                w = cw_ref[t:t + 1, c * LANES:(c + 1) * LANES]
                even = even + w * taps[t]
                odd = odd + w * taps[t + 1]
            y_s[c, pl.ds(r * CONV_ROWS, CONV_ROWS // 2, stride=2), :] = even
            y_s[c, pl.ds(r * CONV_ROWS + 1, CONV_ROWS // 2, stride=2), :] = odd

    y = jnp.concatenate([y_s[c] for c in range(N_SLAB)], axis=1)
    mu = jnp.mean(y, axis=-1, keepdims=True)
    yc = y - mu
    var = jnp.mean(yc * yc, axis=-1, keepdims=True)
    z = yc * lax.rsqrt(var + EPS) * lng_ref[...] + lnb_ref[...]
    cv_s[...] = (z * _sigmoid(z)).astype(BF16)

    head_of_lane = lax.broadcasted_iota(jnp.int32, (BLOCK, GW), 1) // HEAD_DIM
    kmin_first = jnp.where(j == 0, BLOCK, 0)
    for qb in range(tq // BLOCK):
        valid = in_window & (kj >= kmin_first) if qb == 0 else in_window
        rows = slice(qb * BLOCK, (qb + 1) * BLOCK)
        for g in range(N_KV):
            q4 = q_s[rows, g * GW:(g + 1) * GW]
            lhs = jnp.concatenate(
                [jnp.where(head_of_lane == hh, q4, jnp.zeros_like(q4)) for hh in range(GROUP)], axis=0)
            keys = kt_s[g, qb * BLOCK:qb * BLOCK + 2 * BLOCK, :]
            s = lax.dot_general(lhs, keys, (((1,), (1,)), ((), ())), preferred_element_type=F32)
            ps, inv_ls = [], []
            for hh in range(GROUP):
                head = g * GROUP + hh
                sink = sinks_ref[head]
                sh = jnp.where(valid, s[hh * BLOCK:(hh + 1) * BLOCK, :] - alibi_s[head], NEG)
                m = jnp.maximum(jnp.max(sh, axis=-1, keepdims=True), sink)
                p = jnp.exp(sh - m)
                denom = jnp.sum(p, axis=-1, keepdims=True) + jnp.exp(sink - m)
                ps.append(p.astype(BF16))
                inv_ls.append(1.0 / denom)
            pmat = jnp.concatenate(ps, axis=0)
            o = _dot(pmat, vt_s[g, qb * BLOCK:qb * BLOCK + 2 * BLOCK, :])
            out = jnp.zeros((BLOCK, GW), F32)
            for hh in range(GROUP):
                out = jnp.where(head_of_lane == hh, o[hh * BLOCK:(hh + 1) * BLOCK, :] * inv_ls[hh], out)
            attn_s[rows, g * GW:(g + 1) * GW] = out.astype(BF16)

    half = D_MODEL // 2
    for nc in range(2):
        cols = slice(nc * half, (nc + 1) * half)
        br_a = _dot(attn_s[...], wap_ref[:, cols])
        br_c = _dot(cv_s[...], wcp_ref[:, cols]) + bcp_ref[:, cols]
        gate_a = _sigmoid(proj(GTA0 + nc * half, half))
        gate_c = _sigmoid(proj(GTC0 + nc * half, half))
        mg_s[:, cols] = (gate_a * br_a + gate_c * br_c).astype(BF16)
    o_ref[...] = x_ref[...] + _dot(mg_s[...], wout_ref[...])

    kt_s[:, 0:BLOCK, :] = kt_s[:, tq:tq + BLOCK, :]
    vt_s[:, 0:BLOCK, :] = vt_s[:, tq:tq + BLOCK, :]
    u_s[:, 0:UPAD, :] = u_s[:, tq:tq + UPAD, :]


def _mlp_kernel(x_ref, g_ref, w1_ref, w2_ref, gf_ref, o_ref, h_s, a_s, *, final_norm):
    x = x_ref[...]
    ms = jnp.mean(x * x, axis=-1, keepdims=True)
    h_s[...] = (x * lax.rsqrt(ms + EPS) * g_ref[...]).astype(BF16)
    for c in range(D_FF // FF_CHUNK):
        cols = slice(c * FF_CHUNK, (c + 1) * FF_CHUNK)
        a = jnp.maximum(_dot(h_s[...], w1_ref[:, cols]), 0.0)
        a_s[:, cols] = (a * a).astype(BF16)
    y = x_ref[...] + _dot(a_s[...], w2_ref[...])
    if final_norm:
        ms2 = jnp.mean(y * y, axis=-1, keepdims=True)
        y = y * lax.rsqrt(ms2 + EPS) * gf_ref[...]
    o_ref[...] = y


def _resident(shape):
    return pl.BlockSpec(shape, lambda *_: (0,) * len(shape), pipeline_mode=pl.Buffered(1))


def _mixer(x, sinks, g, w_in, b_in, cw, cb, lng, lnb, wap, wcp, bcp, wout):
    b, s, d = x.shape
    tile = pl.BlockSpec((None, TQ, d), lambda bi, ji: (bi, ji, 0))
    return pl.pallas_call(
        _mixer_kernel,
        grid=(b, s // TQ),
        in_specs=[
            pl.BlockSpec(memory_space=pltpu.SMEM),
            tile,
            _resident((1, d)),
            _resident((d, IN_W)),
            _resident((1, IN_W)),
            _resident((CONV_K, CONV_C)),
            _resident((1, CONV_C)),
            _resident((1, CONV_C)),
            _resident((1, CONV_C)),
            _resident((ATTN_W, d)),
            _resident((CONV_C, d)),
            _resident((1, d)),
            _resident((d, d)),
        ],
        out_specs=tile,
        out_shape=jax.ShapeDtypeStruct(x.shape, x.dtype),
        scratch_shapes=[
            pltpu.VMEM((TQ, d), BF16),
            pltpu.VMEM((TQ, ATTN_W), BF16),
            pltpu.VMEM((N_KV, BLOCK + TQ, GW), BF16),
            pltpu.VMEM((N_KV, BLOCK + TQ, GW), BF16),
            pltpu.VMEM((N_SLAB, UPAD + TQ, LANES), F32),
            pltpu.VMEM((N_SLAB, TQ, LANES), F32),
            pltpu.VMEM((TQ, ATTN_W), BF16),
            pltpu.VMEM((TQ, CONV_C), BF16),
            pltpu.VMEM((TQ, d), BF16),
            pltpu.VMEM((N_Q, BLOCK, 2 * BLOCK), F32),
        ],
        compiler_params=pltpu.CompilerParams(
            dimension_semantics=("arbitrary", "arbitrary"), vmem_limit_bytes=VMEM_LIMIT),
        name="mixer",
    )(sinks, x, g, w_in, b_in, cw, cb, lng, lnb, wap, wcp, bcp, wout)


def _mlp(x2, g, w1, w2, gf, final_norm):
    t, d = x2.shape
    tile = pl.BlockSpec((TM, d), lambda i: (i, 0))
    return pl.pallas_call(
        functools.partial(_mlp_kernel, final_norm=final_norm),
        grid=(t // TM,),
        in_specs=[tile, _resident((1, d)), _resident((d, D_FF)), _resident((D_FF, d)), _resident((1, d))],
        out_specs=tile,
        out_shape=jax.ShapeDtypeStruct(x2.shape, x2.dtype),
        scratch_shapes=[pltpu.VMEM((TM, d), BF16), pltpu.VMEM((TM, D_FF), BF16)],
        compiler_params=pltpu.CompilerParams(
            dimension_semantics=("arbitrary",), vmem_limit_bytes=VMEM_LIMIT),
        name="mlp",
    )(x2, g, w1, w2, gf)


def kernel(x, mix_norm_g, w_in, b_in, sinks, conv_w, conv_b, conv_ln_g, conv_ln_b, w_attn_proj, w_conv_proj,
           b_conv_proj, w_out, mlp_norm_g, w_mlp1, w_mlp2, final_norm_g):
    b, s, d = x.shape
    depth = w_in.shape[0]
    assert s % TQ == 0 and (b * s) % TM == 0 and TQ % BLOCK == 0 and TQ % CONV_ROWS == 0
    row = lambda v: v.reshape(1, -1)
    for l in range(depth):
        x = _mixer(x, sinks[l], row(mix_norm_g[l]), w_in[l].astype(BF16), row(b_in[l]), conv_w[l],
                   row(conv_b[l]), row(conv_ln_g[l]), row(conv_ln_b[l]), w_attn_proj[l].astype(BF16),
                   w_conv_proj[l].astype(BF16), row(b_conv_proj[l]), w_out[l].astype(BF16))
        x = _mlp(x.reshape(b * s, d), row(mlp_norm_g[l]), w_mlp1[l].astype(BF16), w_mlp2[l].astype(BF16),
                 row(final_norm_g), final_norm=(l == depth - 1)).reshape(b, s, d)
    return x
```

```python
import functools
import math

import jax
import jax.numpy as jnp
from jax import lax
from jax.experimental import pallas as pl
from jax.experimental.pallas import tpu as pltpu

D_MODEL = 1024
N_Q = 8
N_KV = 2
GROUP = N_Q // N_KV
HEAD_DIM = 64
ATTN_W = N_Q * HEAD_DIM
KV_W = N_KV * HEAD_DIM
WINDOW = 128
BLOCK = 128
CONV_C = D_MODEL // 2
CONV_K = 31
D_FF = 4 * D_MODEL
IN_W = ATTN_W + 2 * KV_W + 2 * CONV_C + 2 * D_MODEL
EPS = 1e-6
NEG = -1e30
SCALE = 1.0 / math.sqrt(HEAD_DIM)
SLOPES = tuple(2.0 ** (-8.0 * h / N_Q) for h in range(1, N_Q + 1))

Q0 = 0
K0 = Q0 + ATTN_W
GA0 = K0 + 2 * KV_W
GB0 = GA0 + CONV_C
GTA0 = GB0 + CONV_C
GTC0 = GTA0 + D_MODEL

LANES = 128
GW = GROUP * HEAD_DIM
N_SLAB = CONV_C // LANES
UPAD = 32
TQ = 512
CONV_ROWS = 64
FF_CHUNK = 512
VMEM_LIMIT = 58 * 1024 * 1024

F32 = jnp.float32
BF16 = jnp.bfloat16


def _dot(a, b):
    return jnp.dot(a, b, preferred_element_type=F32)


def _sigmoid(x):
    return 0.5 * jnp.tanh(0.5 * x) + 0.5


def _rms_norm(x, g):
    ms = jnp.mean(x * x, axis=-1, keepdims=True)
    return x * lax.rsqrt(ms + EPS) * g


def _zero_after(v):
    rows = 32 // jnp.dtype(v.dtype).itemsize
    bits = pltpu.bitcast(v[-rows:, -LANES:], jnp.uint32)
    half = jnp.uint32(16)
    return lax.shift_right_logical(lax.shift_right_logical(bits, half), half).astype(F32)


def _layer_kernel(sinks_ref, x_ref, g_ref, win_ref, bin_ref, cw_ref, cb_ref, lng_ref, lnb_ref,
                  wap_ref, wcp_ref, bcp_ref, wout_ref, g2_ref, w1_ref, w2_ref, gf_ref, o_ref,
                  h_s, q_s, kt_s, vt_s, u_s, y_s, attn_s, cv_s, mg_s, alibi_s, xm_s, h2_s,
                  *, tiles_per_seq, final_norm):
    i = pl.program_id(0)
    j = lax.rem(i, tiles_per_seq)
    tq = x_ref.shape[0]

    qi = lax.broadcasted_iota(jnp.int32, (BLOCK, 2 * BLOCK), 0)
    kj = lax.broadcasted_iota(jnp.int32, (BLOCK, 2 * BLOCK), 1)
    dist = qi + BLOCK - kj
    in_window = (dist >= 0) & (dist < WINDOW)

    @pl.when(i == 0)
    def _():
        xm_s[...] = jnp.zeros(xm_s.shape, F32)
        h2_s[...] = jnp.zeros(h2_s.shape, BF16)
        distf = dist.astype(F32)
        for head in range(N_Q):
            alibi_s[head] = SLOPES[head] * distf

    @pl.when(j == 0)
    def _():
        kt_s[:, 0:BLOCK, :] = jnp.zeros((N_KV, BLOCK, GW), BF16)
        vt_s[:, 0:BLOCK, :] = jnp.zeros((N_KV, BLOCK, GW), BF16)
        u_s[:, 0:UPAD, :] = jnp.zeros((N_SLAB, UPAD, LANES), F32)

    def proj(c0, width):
        return _dot(h_s[...], win_ref[:, c0:c0 + width]) + bin_ref[:, c0:c0 + width]

    def mixer_norm():
        h_s[...] = _rms_norm(x_ref[...], g_ref[...]).astype(BF16)

    def mixer_glu():
        u = proj(GA0, CONV_C) * _sigmoid(proj(GB0, CONV_C))
        for c in range(N_SLAB):
            u_s[c, UPAD:UPAD + tq, :] = u[:, c * LANES:(c + 1) * LANES]

    def mixer_conv(r, after):
        anchor = jnp.tile(after, (CONV_ROWS // 16, 1))
        for c in range(N_SLAB):
            base = r * CONV_ROWS + UPAD - (CONV_K - 1)
            taps = [u_s[c, pl.ds(base + t, CONV_ROWS // 2, stride=2), :] for t in range(CONV_K + 1)]
            bias = cb_ref[:, c * LANES:(c + 1) * LANES] + anchor
            even, odd = bias, bias
            for t in range(CONV_K):
                w = cw_ref[t:t + 1, c * LANES:(c + 1) * LANES]
                even = even + w * taps[t]
                odd = odd + w * taps[t + 1]
            y_s[c, pl.ds(r * CONV_ROWS, CONV_ROWS // 2, stride=2), :] = even
            y_s[c, pl.ds(r * CONV_ROWS + 1, CONV_ROWS // 2, stride=2), :] = odd

    def mixer_conv_act():
        y = jnp.concatenate([y_s[c] for c in range(N_SLAB)], axis=1)
        mu = jnp.mean(y, axis=-1, keepdims=True)
        yc = y - mu
        var = jnp.mean(yc * yc, axis=-1, keepdims=True)
        z = yc * lax.rsqrt(var + EPS) * lng_ref[...] + lnb_ref[...]
        cv_s[...] = (z * _sigmoid(z)).astype(BF16)

    def mixer_qkv():
        q_s[...] = proj(Q0, ATTN_W).astype(BF16)
        kv = proj(K0, 2 * KV_W)
        lo = lax.broadcasted_iota(jnp.int32, (tq, LANES), 1) < HEAD_DIM
        for src, dst in ((kv[:, :KV_W] * SCALE, kt_s), (kv[:, KV_W:], vt_s)):
            rolled = pltpu.roll(src, HEAD_DIM, axis=1)
            g0 = jnp.where(lo, src, rolled).astype(BF16)
            g1 = jnp.where(lo, rolled, src).astype(BF16)
            dst[0, BLOCK:BLOCK + tq, :] = jnp.concatenate([g0, g0], axis=1)
            dst[1, BLOCK:BLOCK + tq, :] = jnp.concatenate([g1, g1], axis=1)

    head_of_lane = lax.broadcasted_iota(jnp.int32, (BLOCK, GW), 1) // HEAD_DIM
    kmin_first = jnp.where(j == 0, BLOCK, 0)

    def mixer_attention(qb, g):
        valid = in_window & (kj >= kmin_first) if qb == 0 else in_window
        rows = slice(qb * BLOCK, (qb + 1) * BLOCK)
        q4 = q_s[rows, g * GW:(g + 1) * GW]
        lhs = jnp.concatenate(
            [jnp.where(head_of_lane == hh, q4, jnp.zeros_like(q4)) for hh in range(GROUP)], axis=0)
        keys = kt_s[g, qb * BLOCK:qb * BLOCK + 2 * BLOCK, :]
        s = lax.dot_general(lhs, keys, (((1,), (1,)), ((), ())), preferred_element_type=F32)
        ps, inv_ls = [], []
        for hh in range(GROUP):
            head = g * GROUP + hh
            sink = sinks_ref[head]
            sh = jnp.where(valid, s[hh * BLOCK:(hh + 1) * BLOCK, :] - alibi_s[head], NEG)
            m = jnp.maximum(jnp.max(sh, axis=-1, keepdims=True), sink)
            p = jnp.exp(sh - m)
            denom = jnp.sum(p, axis=-1, keepdims=True) + jnp.exp(sink - m)
            ps.append(p.astype(BF16))
            inv_ls.append(1.0 / denom)
        pmat = jnp.concatenate(ps, axis=0)
        o = _dot(pmat, vt_s[g, qb * BLOCK:qb * BLOCK + 2 * BLOCK, :])
        out = jnp.zeros((BLOCK, GW), F32)
        for hh in range(GROUP):
            out = jnp.where(head_of_lane == hh, o[hh * BLOCK:(hh + 1) * BLOCK, :] * inv_ls[hh], out)
        attn_s[rows, g * GW:(g + 1) * GW] = out.astype(BF16)

    def mixer_merge(nc):
        half = D_MODEL // 2
        cols = slice(nc * half, (nc + 1) * half)
        br_a = _dot(attn_s[...], wap_ref[:, cols])
        br_c = _dot(cv_s[...], wcp_ref[:, cols]) + bcp_ref[:, cols]
        gate_a = _sigmoid(proj(GTA0 + nc * half, half))
        gate_c = _sigmoid(proj(GTC0 + nc * half, half))
        mg_s[:, cols] = (gate_a * br_a + gate_c * br_c).astype(BF16)

    def mixer_out():
        xm_s[...] = x_ref[...] + _dot(mg_s[...], wout_ref[...])
        kt_s[:, 0:BLOCK, :] = kt_s[:, tq:tq + BLOCK, :]
        vt_s[:, 0:BLOCK, :] = vt_s[:, tq:tq + BLOCK, :]
        u_s[:, 0:UPAD, :] = u_s[:, tq:tq + UPAD, :]

    def mlp_norm():
        h2_s[...] = _rms_norm(xm_s[...], g2_ref[...]).astype(BF16)

    def mlp_up(c):
        cols = slice(c * FF_CHUNK, (c + 1) * FF_CHUNK)
        a = jnp.maximum(_dot(h2_s[...], w1_ref[:, cols]), 0.0)
        return (a * a).astype(BF16)

    def mlp_down(c, a):
        part = _dot(a, w2_ref[c * FF_CHUNK:(c + 1) * FF_CHUNK, :])
        if c == 0:
            o_ref[...] = xm_s[...] + part
        else:
            o_ref[...] += part
        return _zero_after(part)

    n_ff = D_FF // FF_CHUNK
    mixer_norm()
    after = mlp_down(0, mlp_up(0))
    mixer_glu()
    conv_chunk = 0
    for c in range(1, 1 + tq // CONV_ROWS // 2):
        a = mlp_up(c)
        mixer_conv(conv_chunk, after)
        after = _zero_after(a)
        nxt = mlp_down(c, a)
        mixer_conv(conv_chunk + 1, after)
        after = nxt
        conv_chunk += 2
    mixer_qkv()
    mixer_conv_act()
    blocks = [(qb, g) for qb in range(tq // BLOCK) for g in range(N_KV)]
    for c in range(1 + tq // CONV_ROWS // 2, n_ff - 1):
        a = mlp_up(c)
        for qb, g in blocks[:2]:
            mixer_attention(qb, g)
        mlp_down(c, a)
        for qb, g in blocks[2:4]:
            mixer_attention(qb, g)
        blocks = blocks[4:]
    a_last = mlp_up(n_ff - 1)
    mixer_merge(0)
    mixer_merge(1)
    mixer_out()
    mlp_down(n_ff - 1, a_last)
    mlp_norm()
    if final_norm:
        o_ref[...] = _rms_norm(o_ref[...], gf_ref[...])


def _resident(shape):
    return pl.BlockSpec(shape, lambda *_: (0,) * len(shape), pipeline_mode=pl.Buffered(1))


def _layer(x2, tiles_per_seq, final_norm, sinks, g, w_in, b_in, cw, cb, lng, lnb, wap, wcp, bcp, wout, g2, w1, w2, gf):
    t, d = x2.shape
    n_tiles = t // TQ
    assert TQ // CONV_ROWS == 8 and D_FF // FF_CHUNK == 8 and (TQ // BLOCK) * N_KV == 8
    return pl.pallas_call(
        functools.partial(_layer_kernel, tiles_per_seq=tiles_per_seq, final_norm=final_norm),
        grid=(n_tiles + 1,),
        in_specs=[
            pl.BlockSpec(memory_space=pltpu.SMEM),
            pl.BlockSpec((TQ, d), lambda i: (jnp.minimum(i, n_tiles - 1), 0)),
            _resident((1, d)),
            _resident((d, IN_W)),
            _resident((1, IN_W)),
            _resident((CONV_K, CONV_C)),
            _resident((1, CONV_C)),
            _resident((1, CONV_C)),
            _resident((1, CONV_C)),
            _resident((ATTN_W, d)),
            _resident((CONV_C, d)),
            _resident((1, d)),
            _resident((d, d)),
            _resident((1, d)),
            _resident((d, D_FF)),
            _resident((D_FF, d)),
            _resident((1, d)),
        ],
        out_specs=pl.BlockSpec((TQ, d), lambda i: (jnp.maximum(i - 1, 0), 0)),
        out_shape=jax.ShapeDtypeStruct(x2.shape, x2.dtype),
        scratch_shapes=[
            pltpu.VMEM((TQ, d), BF16),
            pltpu.VMEM((TQ, ATTN_W), BF16),
            pltpu.VMEM((N_KV, BLOCK + TQ, GW), BF16),
            pltpu.VMEM((N_KV, BLOCK + TQ, GW), BF16),
            pltpu.VMEM((N_SLAB, UPAD + TQ, LANES), F32),
            pltpu.VMEM((N_SLAB, TQ, LANES), F32),
            pltpu.VMEM((TQ, ATTN_W), BF16),
            pltpu.VMEM((TQ, CONV_C), BF16),
            pltpu.VMEM((TQ, d), BF16),
            pltpu.VMEM((N_Q, BLOCK, 2 * BLOCK), F32),
            pltpu.VMEM((TQ, d), F32),
            pltpu.VMEM((TQ, d), BF16),
        ],
        compiler_params=pltpu.CompilerParams(dimension_semantics=("arbitrary",), vmem_limit_bytes=VMEM_LIMIT),
        name="layer",
    )(sinks, x2, g, w_in, b_in, cw, cb, lng, lnb, wap, wcp, bcp, wout, g2, w1, w2, gf)


def kernel(x, mix_norm_g, w_in, b_in, sinks, conv_w, conv_b, conv_ln_g, conv_ln_b, w_attn_proj, w_conv_proj,
           b_conv_proj, w_out, mlp_norm_g, w_mlp1, w_mlp2, final_norm_g):
    b, s, d = x.shape
    depth = w_in.shape[0]
    assert s % TQ == 0 and TQ % BLOCK == 0 and TQ % CONV_ROWS == 0
    row = lambda v: v.reshape(1, -1)
    x2 = x.reshape(b * s, d)
    for l in range(depth):
        x2 = _layer(x2, s // TQ, l == depth - 1, sinks[l], row(mix_norm_g[l]), w_in[l].astype(BF16), row(b_in[l]),
                    conv_w[l], row(conv_b[l]), row(conv_ln_g[l]), row(conv_ln_b[l]), w_attn_proj[l].astype(BF16),
                    w_conv_proj[l].astype(BF16), row(b_conv_proj[l]), w_out[l].astype(BF16), row(mlp_norm_g[l]),
                    w_mlp1[l].astype(BF16), w_mlp2[l].astype(BF16), row(final_norm_g))
    return x2.reshape(b, s, d)
```

```python
import functools
import math

import jax
import jax.numpy as jnp
from jax import lax
from jax.experimental import pallas as pl
from jax.experimental.pallas import tpu as pltpu

D_MODEL = 1024
N_Q = 8
N_KV = 2
GROUP = N_Q // N_KV
HEAD_DIM = 64
ATTN_W = N_Q * HEAD_DIM
KV_W = N_KV * HEAD_DIM
WINDOW = 128
BLOCK = 128
CONV_C = D_MODEL // 2
CONV_K = 31
D_FF = 4 * D_MODEL
IN_W = ATTN_W + 2 * KV_W + 2 * CONV_C + 2 * D_MODEL
EPS = 1e-6
NEG = -1e30
SCALE = 1.0 / math.sqrt(HEAD_DIM)
SLOPES = tuple(2.0 ** (-8.0 * h / N_Q) for h in range(1, N_Q + 1))

Q0 = 0
K0 = Q0 + ATTN_W
GA0 = K0 + 2 * KV_W
GB0 = GA0 + CONV_C
GTA0 = GB0 + CONV_C
GTC0 = GTA0 + D_MODEL

LANES = 128
GW = GROUP * HEAD_DIM
N_SLAB = CONV_C // LANES
UPAD = 32
TQ = 512
CONV_ROWS = 64
TAP_GROUP = 4
FF_CHUNK = 512
VMEM_LIMIT = 58 * 1024 * 1024

F32 = jnp.float32
BF16 = jnp.bfloat16


def _dot(a, b):
    return jnp.dot(a, b, preferred_element_type=F32)


def _sigmoid(x):
    return 0.5 * jnp.tanh(0.5 * x) + 0.5


def _rms_norm(x, g):
    ms = jnp.mean(x * x, axis=-1, keepdims=True)
    return x * lax.rsqrt(ms + EPS) * g


def _zero_after(v):
    rows = 32 // jnp.dtype(v.dtype).itemsize
    bits = pltpu.bitcast(v[-rows:, -LANES:], jnp.uint32)
    half = jnp.uint32(16)
    return lax.shift_right_logical(lax.shift_right_logical(bits, half), half).astype(F32)


def _layer_kernel(sinks_ref, x_ref, g_ref, win_ref, bin_ref, cw_ref, cb_ref, lng_ref, lnb_ref,
                  wap_ref, wcp_ref, bcp_ref, wout_ref, g2_ref, w1_ref, w2_ref, gf_ref, o_ref,
                  h_s, q_s, kt_s, vt_s, u_s, y_s, attn_s, cv_s, mg_s, alibi_s, xm_s, h2_s,
                  *, tiles_per_seq, final_norm):
    i = pl.program_id(0)
    j = lax.rem(i, tiles_per_seq)
    tq = x_ref.shape[0]

    qi = lax.broadcasted_iota(jnp.int32, (BLOCK, 2 * BLOCK), 0)
    kj = lax.broadcasted_iota(jnp.int32, (BLOCK, 2 * BLOCK), 1)
    dist = qi + BLOCK - kj
    in_window = (dist >= 0) & (dist < WINDOW)

    @pl.when(i == 0)
    def _():
        xm_s[...] = jnp.zeros(xm_s.shape, F32)
        h2_s[...] = jnp.zeros(h2_s.shape, BF16)
        distf = dist.astype(F32)
        for head in range(N_Q):
            alibi_s[head] = SLOPES[head] * distf

    @pl.when(j == 0)
    def _():
        kt_s[:, 0:BLOCK, :] = jnp.zeros((N_KV, BLOCK, GW), BF16)
        vt_s[:, 0:BLOCK, :] = jnp.zeros((N_KV, BLOCK, GW), BF16)
        u_s[:, 0:UPAD, :] = jnp.zeros((N_SLAB, UPAD, LANES), F32)

    def proj(c0, width):
        return _dot(h_s[...], win_ref[:, c0:c0 + width]) + bin_ref[:, c0:c0 + width]

    def mixer_norm():
        h_s[...] = _rms_norm(x_ref[...], g_ref[...]).astype(BF16)

    def mixer_glu():
        u = proj(GA0, CONV_C) * _sigmoid(proj(GB0, CONV_C))
        for c in range(N_SLAB):
            u_s[c, UPAD:UPAD + tq, :] = u[:, c * LANES:(c + 1) * LANES]

    def mixer_conv(r, after):
        n_vreg = CONV_ROWS // 16
        for c in range(N_SLAB):
            lanes = slice(c * LANES, (c + 1) * LANES)
            base = r * CONV_ROWS + UPAD - (CONV_K - 1)
            gate = after
            even = odd = cb_ref[:, lanes] + jnp.tile(after, (n_vreg, 1))
            w_prev = None
            for t in range(CONV_K + 1):
                rows = u_s[c, pl.ds(base + t, CONV_ROWS // 2, stride=2), :]
                if t > 0:
                    odd = odd + w_prev * rows
                if t < CONV_K:
                    if t > 0 and t % TAP_GROUP == 0:
                        gate = _zero_after(even)
                    w_prev = jnp.tile(cw_ref[t:t + 1, lanes] + gate, (n_vreg, 1))
                    even = even + w_prev * rows
            y_s[c, pl.ds(r * CONV_ROWS, CONV_ROWS // 2, stride=2), :] = even
            y_s[c, pl.ds(r * CONV_ROWS + 1, CONV_ROWS // 2, stride=2), :] = odd

    def mixer_conv_act():
        y = jnp.concatenate([y_s[c] for c in range(N_SLAB)], axis=1)
        mu = jnp.mean(y, axis=-1, keepdims=True)
        yc = y - mu
        var = jnp.mean(yc * yc, axis=-1, keepdims=True)
        z = yc * lax.rsqrt(var + EPS) * lng_ref[...] + lnb_ref[...]
        cv_s[...] = (z * _sigmoid(z)).astype(BF16)

    def mixer_qkv():
        q_s[...] = proj(Q0, ATTN_W).astype(BF16)
        kv = proj(K0, 2 * KV_W)
        lo = lax.broadcasted_iota(jnp.int32, (tq, LANES), 1) < HEAD_DIM
        for src, dst in ((kv[:, :KV_W] * SCALE, kt_s), (kv[:, KV_W:], vt_s)):
            rolled = pltpu.roll(src, HEAD_DIM, axis=1)
            g0 = jnp.where(lo, src, rolled).astype(BF16)
            g1 = jnp.where(lo, rolled, src).astype(BF16)
            dst[0, BLOCK:BLOCK + tq, :] = jnp.concatenate([g0, g0], axis=1)
            dst[1, BLOCK:BLOCK + tq, :] = jnp.concatenate([g1, g1], axis=1)

    head_of_lane = lax.broadcasted_iota(jnp.int32, (BLOCK, GW), 1) // HEAD_DIM
    kmin_first = jnp.where(j == 0, BLOCK, 0)

    def mixer_attention(qb, g):
        valid = in_window & (kj >= kmin_first) if qb == 0 else in_window
        rows = slice(qb * BLOCK, (qb + 1) * BLOCK)
        q4 = q_s[rows, g * GW:(g + 1) * GW]
        lhs = jnp.concatenate(
            [jnp.where(head_of_lane == hh, q4, jnp.zeros_like(q4)) for hh in range(GROUP)], axis=0)
        keys = kt_s[g, qb * BLOCK:qb * BLOCK + 2 * BLOCK, :]
        s = lax.dot_general(lhs, keys, (((1,), (1,)), ((), ())), preferred_element_type=F32)
        ps, inv_ls = [], []
        for hh in range(GROUP):
            head = g * GROUP + hh
            sink = sinks_ref[head]
            sh = jnp.where(valid, s[hh * BLOCK:(hh + 1) * BLOCK, :] - alibi_s[head], NEG)
            m = jnp.maximum(jnp.max(sh, axis=-1, keepdims=True), sink)
            p = jnp.exp(sh - m)
            denom = jnp.sum(p, axis=-1, keepdims=True) + jnp.exp(sink - m)
            ps.append(p.astype(BF16))
            inv_ls.append(1.0 / denom)
        pmat = jnp.concatenate(ps, axis=0)
        o = _dot(pmat, vt_s[g, qb * BLOCK:qb * BLOCK + 2 * BLOCK, :])
        out = jnp.zeros((BLOCK, GW), F32)
        for hh in range(GROUP):
            out = jnp.where(head_of_lane == hh, o[hh * BLOCK:(hh + 1) * BLOCK, :] * inv_ls[hh], out)
        attn_s[rows, g * GW:(g + 1) * GW] = out.astype(BF16)

    def mixer_merge(nc):
        half = D_MODEL // 2
        cols = slice(nc * half, (nc + 1) * half)
        br_a = _dot(attn_s[...], wap_ref[:, cols])
        br_c = _dot(cv_s[...], wcp_ref[:, cols]) + bcp_ref[:, cols]
        gate_a = _sigmoid(proj(GTA0 + nc * half, half))
        gate_c = _sigmoid(proj(GTC0 + nc * half, half))
        mg_s[:, cols] = (gate_a * br_a + gate_c * br_c).astype(BF16)

    def mixer_out():
        xm_s[...] = x_ref[...] + _dot(mg_s[...], wout_ref[...])
        kt_s[:, 0:BLOCK, :] = kt_s[:, tq:tq + BLOCK, :]
        vt_s[:, 0:BLOCK, :] = vt_s[:, tq:tq + BLOCK, :]
        u_s[:, 0:UPAD, :] = u_s[:, tq:tq + UPAD, :]

    def mlp_norm():
        h2_s[...] = _rms_norm(xm_s[...], g2_ref[...]).astype(BF16)

    def mlp_up(c):
        cols = slice(c * FF_CHUNK, (c + 1) * FF_CHUNK)
        a = jnp.maximum(_dot(h2_s[...], w1_ref[:, cols]), 0.0)
        return (a * a).astype(BF16)

    def mlp_down(c, a):
        part = _dot(a, w2_ref[c * FF_CHUNK:(c + 1) * FF_CHUNK, :])
        if c == 0:
            o_ref[...] = xm_s[...] + part
        else:
            o_ref[...] += part
        return _zero_after(part)

    n_ff = D_FF // FF_CHUNK
    mixer_norm()
    after = mlp_down(0, mlp_up(0))
    mixer_glu()
    conv_chunk = 0
    for c in range(1, 1 + tq // CONV_ROWS // 2):
        a = mlp_up(c)
        mixer_conv(conv_chunk, after)
        after = _zero_after(a)
        nxt = mlp_down(c, a)
        mixer_conv(conv_chunk + 1, after)
        after = nxt
        conv_chunk += 2
    mixer_qkv()
    mixer_conv_act()
    blocks = [(qb, g) for qb in range(tq // BLOCK) for g in range(N_KV)]
    for c in range(1 + tq // CONV_ROWS // 2, n_ff - 1):
        a = mlp_up(c)
        for qb, g in blocks[:2]:
            mixer_attention(qb, g)
        mlp_down(c, a)
        for qb, g in blocks[2:4]:
            mixer_attention(qb, g)
        blocks = blocks[4:]
    a_last = mlp_up(n_ff - 1)
    mixer_merge(0)
    mixer_merge(1)
    mixer_out()
    mlp_down(n_ff - 1, a_last)
    mlp_norm()
    if final_norm:
        o_ref[...] = _rms_norm(o_ref[...], gf_ref[...])


def _resident(shape):
    return pl.BlockSpec(shape, lambda *_: (0,) * len(shape), pipeline_mode=pl.Buffered(1))


def _layer(x2, tiles_per_seq, final_norm, sinks, g, w_in, b_in, cw, cb, lng, lnb, wap, wcp, bcp, wout, g2, w1, w2, gf):
    t, d = x2.shape
    n_tiles = t // TQ
    assert TQ // CONV_ROWS == 8 and D_FF // FF_CHUNK == 8 and (TQ // BLOCK) * N_KV == 8
    return pl.pallas_call(
        functools.partial(_layer_kernel, tiles_per_seq=tiles_per_seq, final_norm=final_norm),
        grid=(n_tiles + 1,),
        in_specs=[
            pl.BlockSpec(memory_space=pltpu.SMEM),
            pl.BlockSpec((TQ, d), lambda i: (jnp.minimum(i, n_tiles - 1), 0)),
            _resident((1, d)),
            _resident((d, IN_W)),
            _resident((1, IN_W)),
            _resident((CONV_K, CONV_C)),
            _resident((1, CONV_C)),
            _resident((1, CONV_C)),
            _resident((1, CONV_C)),
            _resident((ATTN_W, d)),
            _resident((CONV_C, d)),
            _resident((1, d)),
            _resident((d, d)),
            _resident((1, d)),
            _resident((d, D_FF)),
            _resident((D_FF, d)),
            _resident((1, d)),
        ],
        out_specs=pl.BlockSpec((TQ, d), lambda i: (jnp.maximum(i - 1, 0), 0)),
        out_shape=jax.ShapeDtypeStruct(x2.shape, x2.dtype),
        scratch_shapes=[
            pltpu.VMEM((TQ, d), BF16),
            pltpu.VMEM((TQ, ATTN_W), BF16),
            pltpu.VMEM((N_KV, BLOCK + TQ, GW), BF16),
            pltpu.VMEM((N_KV, BLOCK + TQ, GW), BF16),
            pltpu.VMEM((N_SLAB, UPAD + TQ, LANES), F32),
            pltpu.VMEM((N_SLAB, TQ, LANES), F32),
            pltpu.VMEM((TQ, ATTN_W), BF16),
            pltpu.VMEM((TQ, CONV_C), BF16),
            pltpu.VMEM((TQ, d), BF16),
            pltpu.VMEM((N_Q, BLOCK, 2 * BLOCK), F32),
            pltpu.VMEM((TQ, d), F32),
            pltpu.VMEM((TQ, d), BF16),
        ],
        compiler_params=pltpu.CompilerParams(dimension_semantics=("arbitrary",), vmem_limit_bytes=VMEM_LIMIT),
        name="layer",
    )(sinks, x2, g, w_in, b_in, cw, cb, lng, lnb, wap, wcp, bcp, wout, g2, w1, w2, gf)


def kernel(x, mix_norm_g, w_in, b_in, sinks, conv_w, conv_b, conv_ln_g, conv_ln_b, w_attn_proj, w_conv_proj,
           b_conv_proj, w_out, mlp_norm_g, w_mlp1, w_mlp2, final_norm_g):
    b, s, d = x.shape
    depth = w_in.shape[0]
    assert s % TQ == 0 and TQ % BLOCK == 0 and TQ % CONV_ROWS == 0
    row = lambda v: v.reshape(1, -1)
    x2 = x.reshape(b * s, d)
    for l in range(depth):
        x2 = _layer(x2, s // TQ, l == depth - 1, sinks[l], row(mix_norm_g[l]), w_in[l].astype(BF16), row(b_in[l]),
                    conv_w[l], row(conv_b[l]), row(conv_ln_g[l]), row(conv_ln_b[l]), w_attn_proj[l].astype(BF16),
                    w_conv_proj[l].astype(BF16), row(b_conv_proj[l]), w_out[l].astype(BF16), row(mlp_norm_g[l]),
                    w_mlp1[l].astype(BF16), w_mlp2[l].astype(BF16), row(final_norm_g))
    return x2.reshape(b, s, d)
```

```python
import functools
import math

import jax
import jax.numpy as jnp
from jax import lax
from jax.experimental import pallas as pl
from jax.experimental.pallas import tpu as pltpu

D_MODEL = 1024
N_Q = 8
N_KV = 2
GROUP = N_Q // N_KV
HEAD_DIM = 64
ATTN_W = N_Q * HEAD_DIM
KV_W = N_KV * HEAD_DIM
WINDOW = 128
BLOCK = 128
CONV_C = D_MODEL // 2
CONV_K = 31
D_FF = 4 * D_MODEL
IN_W = ATTN_W + 2 * KV_W + 2 * CONV_C + 2 * D_MODEL
EPS = 1e-6
NEG = -1e30
SCALE = 1.0 / math.sqrt(HEAD_DIM)
SLOPES = tuple(2.0 ** (-8.0 * h / N_Q) for h in range(1, N_Q + 1))

Q0 = 0
K0 = Q0 + ATTN_W
GA0 = K0 + 2 * KV_W
GB0 = GA0 + CONV_C
GTA0 = GB0 + CONV_C
GTC0 = GTA0 + D_MODEL

LANES = 128
GW = GROUP * HEAD_DIM
N_SLAB = CONV_C // LANES
UPAD = 32
TQ = 512
CONV_ROWS = 64
TAP_GROUP = 4
FF_CHUNK = 512
VMEM_LIMIT = 58 * 1024 * 1024

F32 = jnp.float32
BF16 = jnp.bfloat16


def _dot(a, b):
    return jnp.dot(a, b, preferred_element_type=F32)


def _sigmoid(x):
    return 0.5 * jnp.tanh(0.5 * x) + 0.5


def _rms_norm(x, g):
    ms = jnp.mean(x * x, axis=-1, keepdims=True)
    return x * lax.rsqrt(ms + EPS) * g


def _zero_after(v):
    rows = 32 // jnp.dtype(v.dtype).itemsize
    bits = pltpu.bitcast(v[-rows:, -LANES:], jnp.uint32)
    half = jnp.uint32(16)
    return lax.shift_right_logical(lax.shift_right_logical(bits, half), half).astype(F32)


def _layer_kernel(sinks_ref, x_ref, g_ref, win_ref, bin_ref, cw_ref, cb_ref, lng_ref, lnb_ref,
                  wap_ref, wcp_ref, bcp_ref, wout_ref, g2_ref, w1_ref, w2_ref, gf_ref, o_ref,
                  h_s, q_s, kt_s, vt_s, u_s, y_s, attn_s, cv_s, mg_s, alibi_s, xm_s, h2_s, wglu_s,
                  *, tiles_per_seq, final_norm):
    i = pl.program_id(0)
    j = lax.rem(i, tiles_per_seq)
    tq = x_ref.shape[0]

    qi = lax.broadcasted_iota(jnp.int32, (BLOCK, 2 * BLOCK), 0)
    kj = lax.broadcasted_iota(jnp.int32, (BLOCK, 2 * BLOCK), 1)
    dist = qi + BLOCK - kj
    in_window = (dist >= 0) & (dist < WINDOW)

    @pl.when(i == 0)
    def _():
        xm_s[...] = jnp.zeros(xm_s.shape, F32)
        h2_s[...] = jnp.zeros(h2_s.shape, BF16)
        distf = dist.astype(F32)
        for head in range(N_Q):
            alibi_s[head] = SLOPES[head] * distf
        for c in range(N_SLAB):
            wglu_s[:, 2 * c * LANES:(2 * c + 1) * LANES] = win_ref[:, GA0 + c * LANES:GA0 + (c + 1) * LANES]
            wglu_s[:, (2 * c + 1) * LANES:(2 * c + 2) * LANES] = win_ref[:, GB0 + c * LANES:GB0 + (c + 1) * LANES]

    @pl.when(j == 0)
    def _():
        kt_s[:, 0:BLOCK, :] = jnp.zeros((N_KV, BLOCK, GW), BF16)
        vt_s[:, 0:BLOCK, :] = jnp.zeros((N_KV, BLOCK, GW), BF16)
        u_s[:, 0:UPAD, :] = jnp.zeros((N_SLAB, UPAD, LANES), F32)

    def proj(c0, width):
        return _dot(h_s[...], win_ref[:, c0:c0 + width]) + bin_ref[:, c0:c0 + width]

    def mixer_norm():
        h_s[...] = _rms_norm(x_ref[...], g_ref[...]).astype(BF16)

    def mixer_glu():
        res = _dot(h_s[...], wglu_s[...])
        for c in range(N_SLAB):
            val = res[:, 2 * c * LANES:(2 * c + 1) * LANES] + bin_ref[:, GA0 + c * LANES:GA0 + (c + 1) * LANES]
            gate = res[:, (2 * c + 1) * LANES:(2 * c + 2) * LANES] + bin_ref[:, GB0 + c * LANES:GB0 + (c + 1) * LANES]
            u_s[c, UPAD:UPAD + tq, :] = val * _sigmoid(gate)

    def mixer_conv(r, after):
        n_vreg = CONV_ROWS // 16
        for c in range(N_SLAB):
            lanes = slice(c * LANES, (c + 1) * LANES)
            base = r * CONV_ROWS + UPAD - (CONV_K - 1)
            gate = after
            even = odd = cb_ref[:, lanes] + jnp.tile(after, (n_vreg, 1))
            w_prev = None
            for t in range(CONV_K + 1):
                rows = u_s[c, pl.ds(base + t, CONV_ROWS // 2, stride=2), :]
                if t > 0:
                    odd = odd + w_prev * rows
                if t < CONV_K:
                    if t > 0 and t % TAP_GROUP == 0:
                        gate = _zero_after(even)
                    w_prev = jnp.tile(cw_ref[t:t + 1, lanes] + gate, (n_vreg, 1))
                    even = even + w_prev * rows
            y_s[c, pl.ds(r * CONV_ROWS, CONV_ROWS // 2, stride=2), :] = even
            y_s[c, pl.ds(r * CONV_ROWS + 1, CONV_ROWS // 2, stride=2), :] = odd

    def mixer_conv_act():
        y = jnp.concatenate([y_s[c] for c in range(N_SLAB)], axis=1)
        mu = jnp.mean(y, axis=-1, keepdims=True)
        yc = y - mu
        var = jnp.mean(yc * yc, axis=-1, keepdims=True)
        z = yc * lax.rsqrt(var + EPS) * lng_ref[...] + lnb_ref[...]
        cv_s[...] = (z * _sigmoid(z)).astype(BF16)

    def mixer_qkv():
        qkv = proj(Q0, ATTN_W + 2 * KV_W)
        q_s[...] = qkv[:, :ATTN_W].astype(BF16)
        kv = qkv[:, ATTN_W:]
        lo = lax.broadcasted_iota(jnp.int32, (tq, LANES), 1) < HEAD_DIM
        for src, dst in ((kv[:, :KV_W] * SCALE, kt_s), (kv[:, KV_W:], vt_s)):
            rolled = pltpu.roll(src, HEAD_DIM, axis=1)
            g0 = jnp.where(lo, src, rolled).astype(BF16)
            g1 = jnp.where(lo, rolled, src).astype(BF16)
            dst[0, BLOCK:BLOCK + tq, :] = jnp.concatenate([g0, g0], axis=1)
            dst[1, BLOCK:BLOCK + tq, :] = jnp.concatenate([g1, g1], axis=1)

    head_of_lane = lax.broadcasted_iota(jnp.int32, (BLOCK, GW), 1) // HEAD_DIM
    kmin_first = jnp.where(j == 0, BLOCK, 0)

    def mixer_scores(qb, g):
        valid = in_window & (kj >= kmin_first) if qb == 0 else in_window
        rows = slice(qb * BLOCK, (qb + 1) * BLOCK)
        q4 = q_s[rows, g * GW:(g + 1) * GW]
        lhs = jnp.concatenate(
            [jnp.where(head_of_lane == hh, q4, jnp.zeros_like(q4)) for hh in range(GROUP)], axis=0)
        keys = kt_s[g, qb * BLOCK:qb * BLOCK + 2 * BLOCK, :]
        s = lax.dot_general(lhs, keys, (((1,), (1,)), ((), ())), preferred_element_type=F32)
        ps, inv_ls = [], []
        for hh in range(GROUP):
            head = g * GROUP + hh
            sink = sinks_ref[head]
            sh = jnp.where(valid, s[hh * BLOCK:(hh + 1) * BLOCK, :] - alibi_s[head], NEG)
            m = jnp.maximum(jnp.max(sh, axis=-1, keepdims=True), sink)
            p = jnp.exp(sh - m)
            denom = jnp.sum(p, axis=-1, keepdims=True) + jnp.exp(sink - m)
            ps.append(p.astype(BF16))
            inv_ls.append(1.0 / denom)
        return qb, g, jnp.concatenate(ps, axis=0), inv_ls

    def mixer_attend(qb, g, pmat, inv_ls):
        rows = slice(qb * BLOCK, (qb + 1) * BLOCK)
        o = _dot(pmat, vt_s[g, qb * BLOCK:qb * BLOCK + 2 * BLOCK, :])
        out = jnp.zeros((BLOCK, GW), F32)
        for hh in range(GROUP):
            out = jnp.where(head_of_lane == hh, o[hh * BLOCK:(hh + 1) * BLOCK, :] * inv_ls[hh], out)
        attn_s[rows, g * GW:(g + 1) * GW] = out.astype(BF16)

    def mixer_merge(nc):
        half = D_MODEL // 2
        cols = slice(nc * half, (nc + 1) * half)
        gate_a = _sigmoid(proj(GTA0 + nc * half, half))
        gate_c = _sigmoid(proj(GTC0 + nc * half, half))
        br_c = _dot(cv_s[...], wcp_ref[:, cols]) + bcp_ref[:, cols]
        br_a = _dot(attn_s[...], wap_ref[:, cols])
        mg_s[:, cols] = (gate_a * br_a + gate_c * br_c).astype(BF16)

    def mixer_out():
        xm_s[...] = x_ref[...] + _dot(mg_s[...], wout_ref[...])
        kt_s[:, 0:BLOCK, :] = kt_s[:, tq:tq + BLOCK, :]
        vt_s[:, 0:BLOCK, :] = vt_s[:, tq:tq + BLOCK, :]
        u_s[:, 0:UPAD, :] = u_s[:, tq:tq + UPAD, :]

    def mlp_norm():
        h2_s[...] = _rms_norm(xm_s[...], g2_ref[...]).astype(BF16)

    def mlp_up(c):
        cols = slice(c * FF_CHUNK, (c + 1) * FF_CHUNK)
        a = jnp.maximum(_dot(h2_s[...], w1_ref[:, cols]), 0.0)
        return (a * a).astype(BF16)

    def mlp_down(c, a):
        part = _dot(a, w2_ref[c * FF_CHUNK:(c + 1) * FF_CHUNK, :])
        if c == 0:
            o_ref[...] = xm_s[...] + part
        else:
            o_ref[...] += part
        return _zero_after(part)

    n_ff = D_FF // FF_CHUNK
    mixer_norm()
    after = mlp_down(0, mlp_up(0))
    mixer_glu()
    conv_chunk = 0
    for c in range(1, 1 + tq // CONV_ROWS // 2):
        a = mlp_up(c)
        mixer_conv(conv_chunk, after)
        after = _zero_after(a)
        nxt = mlp_down(c, a)
        mixer_conv(conv_chunk + 1, after)
        after = nxt
        conv_chunk += 2
    mixer_qkv()
    mixer_conv_act()
    blocks = [(qb, g) for qb in range(tq // BLOCK) for g in range(N_KV)]
    pending = [mixer_scores(*blk) for blk in blocks[0:2]]
    mlp_chunks = []
    for c in range(1 + tq // CONV_ROWS // 2, n_ff - 1):
        mlp_chunks += [("up", c), ("down", c)]
    mlp_chunks.append(("up", n_ff - 1))
    a = None
    for k, (kind, c) in enumerate(mlp_chunks):
        if kind == "up":
            a = mlp_up(c)
        else:
            mlp_down(c, a)
        for scored in pending:
            mixer_attend(*scored)
        pending = [mixer_scores(*blk) for blk in blocks[2 * k + 2:2 * k + 4]]
    a_last = a
    mixer_merge(0)
    mixer_merge(1)
    mixer_out()
    mlp_down(n_ff - 1, a_last)
    mlp_norm()
    if final_norm:
        o_ref[...] = _rms_norm(o_ref[...], gf_ref[...])


def _resident(shape):
    return pl.BlockSpec(shape, lambda *_: (0,) * len(shape), pipeline_mode=pl.Buffered(1))


def _layer(x2, tiles_per_seq, final_norm, sinks, g, w_in, b_in, cw, cb, lng, lnb, wap, wcp, bcp, wout, g2, w1, w2, gf):
    t, d = x2.shape
    n_tiles = t // TQ
    assert TQ // CONV_ROWS == 8 and D_FF // FF_CHUNK == 8 and (TQ // BLOCK) * N_KV == 8
    return pl.pallas_call(
        functools.partial(_layer_kernel, tiles_per_seq=tiles_per_seq, final_norm=final_norm),
        grid=(n_tiles + 1,),
        in_specs=[
            pl.BlockSpec(memory_space=pltpu.SMEM),
            pl.BlockSpec((TQ, d), lambda i: (jnp.minimum(i, n_tiles - 1), 0)),
            _resident((1, d)),
            _resident((d, IN_W)),
            _resident((1, IN_W)),
            _resident((CONV_K, CONV_C)),
            _resident((1, CONV_C)),
            _resident((1, CONV_C)),
            _resident((1, CONV_C)),
            _resident((ATTN_W, d)),
            _resident((CONV_C, d)),
            _resident((1, d)),
            _resident((d, d)),
            _resident((1, d)),
            _resident((d, D_FF)),
            _resident((D_FF, d)),
            _resident((1, d)),
        ],
        out_specs=pl.BlockSpec((TQ, d), lambda i: (jnp.maximum(i - 1, 0), 0)),
        out_shape=jax.ShapeDtypeStruct(x2.shape, x2.dtype),
        scratch_shapes=[
            pltpu.VMEM((TQ, d), BF16),
            pltpu.VMEM((TQ, ATTN_W), BF16),
            pltpu.VMEM((N_KV, BLOCK + TQ, GW), BF16),
            pltpu.VMEM((N_KV, BLOCK + TQ, GW), BF16),
            pltpu.VMEM((N_SLAB, UPAD + TQ, LANES), F32),
            pltpu.VMEM((N_SLAB, TQ, LANES), F32),
            pltpu.VMEM((TQ, ATTN_W), BF16),
            pltpu.VMEM((TQ, CONV_C), BF16),
            pltpu.VMEM((TQ, d), BF16),
            pltpu.VMEM((N_Q, BLOCK, 2 * BLOCK), F32),
            pltpu.VMEM((TQ, d), F32),
            pltpu.VMEM((TQ, d), BF16),
            pltpu.VMEM((d, 2 * CONV_C), BF16),
        ],
        compiler_params=pltpu.CompilerParams(dimension_semantics=("arbitrary",), vmem_limit_bytes=VMEM_LIMIT),
        name="layer",
    )(sinks, x2, g, w_in, b_in, cw, cb, lng, lnb, wap, wcp, bcp, wout, g2, w1, w2, gf)


def kernel(x, mix_norm_g, w_in, b_in, sinks, conv_w, conv_b, conv_ln_g, conv_ln_b, w_attn_proj, w_conv_proj,
           b_conv_proj, w_out, mlp_norm_g, w_mlp1, w_mlp2, final_norm_g):
    b, s, d = x.shape
    depth = w_in.shape[0]
    assert s % TQ == 0 and TQ % BLOCK == 0 and TQ % CONV_ROWS == 0
    row = lambda v: v.reshape(1, -1)
    x2 = x.reshape(b * s, d)
    for l in range(depth):
        x2 = _layer(x2, s // TQ, l == depth - 1, sinks[l], row(mix_norm_g[l]), w_in[l].astype(BF16), row(b_in[l]),
                    conv_w[l], row(conv_b[l]), row(conv_ln_g[l]), row(conv_ln_b[l]), w_attn_proj[l].astype(BF16),
                    w_conv_proj[l].astype(BF16), row(b_conv_proj[l]), w_out[l].astype(BF16), row(mlp_norm_g[l]),
                    w_mlp1[l].astype(BF16), w_mlp2[l].astype(BF16), row(final_norm_g))
    return x2.reshape(b, s, d)
```

```python
import functools
import math

import jax
import jax.numpy as jnp
from jax import lax
from jax.experimental import pallas as pl
from jax.experimental.pallas import tpu as pltpu

D_MODEL = 1024
N_Q = 8
N_KV = 2
GROUP = N_Q // N_KV
HEAD_DIM = 64
ATTN_W = N_Q * HEAD_DIM
KV_W = N_KV * HEAD_DIM
WINDOW = 128
BLOCK = 128
CONV_C = D_MODEL // 2
CONV_K = 31
D_FF = 4 * D_MODEL
IN_W = ATTN_W + 2 * KV_W + 2 * CONV_C + 2 * D_MODEL
EPS = 1e-6
NEG = -1e30
SCALE = 1.0 / math.sqrt(HEAD_DIM)
SLOPES = tuple(2.0 ** (-8.0 * h / N_Q) for h in range(1, N_Q + 1))

Q0 = 0
K0 = Q0 + ATTN_W
GA0 = K0 + 2 * KV_W
GB0 = GA0 + CONV_C
GTA0 = GB0 + CONV_C
GTC0 = GTA0 + D_MODEL

LANES = 128
GW = GROUP * HEAD_DIM
N_SLAB = CONV_C // LANES
UPAD = 32
TQ = 512
CONV_ROWS = 64
TAP_GROUP = 4
FF_CHUNK = 512
VMEM_LIMIT = 58 * 1024 * 1024

F32 = jnp.float32
BF16 = jnp.bfloat16


def _dot(a, b):
    return jnp.dot(a, b, preferred_element_type=F32)


def _sigmoid(x):
    return 0.5 * jnp.tanh(0.5 * x) + 0.5


def _rms_norm(x, g):
    ms = jnp.mean(x * x, axis=-1, keepdims=True)
    return x * lax.rsqrt(ms + EPS) * g


def _zero_after(v):
    rows = 32 // jnp.dtype(v.dtype).itemsize
    bits = pltpu.bitcast(v[-rows:, -LANES:], jnp.uint32)
    half = jnp.uint32(16)
    return lax.shift_right_logical(lax.shift_right_logical(bits, half), half).astype(F32)


def _layer_kernel(sinks_ref, x_ref, g_ref, win_ref, bin_ref, cw_ref, cb_ref, lng_ref, lnb_ref,
                  wap_ref, wcp_ref, bcp_ref, wout_ref, g2_ref, w1_ref, w2_ref, gf_ref, o_ref,
                  h_s, q_s, kt_s, vt_s, u_s, y_s, attn_s, cv_s, mg_s, alibi_s, xm_s, h2_s, wglu_s,
                  *, tiles_per_seq, final_norm):
    i = pl.program_id(0)
    j = lax.rem(i, tiles_per_seq)
    tq = x_ref.shape[0]

    qi = lax.broadcasted_iota(jnp.int32, (BLOCK, 2 * BLOCK), 0)
    kj = lax.broadcasted_iota(jnp.int32, (BLOCK, 2 * BLOCK), 1)
    dist = qi + BLOCK - kj
    in_window = (dist >= 0) & (dist < WINDOW)

    @pl.when(i == 0)
    def _():
        xm_s[...] = jnp.zeros(xm_s.shape, F32)
        h2_s[...] = jnp.zeros(h2_s.shape, BF16)
        distf = dist.astype(F32)
        for head in range(N_Q):
            alibi_s[head] = SLOPES[head] * distf
        for c in range(N_SLAB):
            wglu_s[:, 2 * c * LANES:(2 * c + 1) * LANES] = win_ref[:, GA0 + c * LANES:GA0 + (c + 1) * LANES]
            wglu_s[:, (2 * c + 1) * LANES:(2 * c + 2) * LANES] = win_ref[:, GB0 + c * LANES:GB0 + (c + 1) * LANES]

    @pl.when(j == 0)
    def _():
        kt_s[:, 0:BLOCK, :] = jnp.zeros((N_KV, BLOCK, GW), BF16)
        vt_s[:, 0:BLOCK, :] = jnp.zeros((N_KV, BLOCK, GW), BF16)
        u_s[:, 0:UPAD, :] = jnp.zeros((N_SLAB, UPAD, LANES), F32)

    def proj(c0, width):
        return _dot(h_s[...], win_ref[:, c0:c0 + width]) + bin_ref[:, c0:c0 + width]

    def mixer_norm():
        h_s[...] = _rms_norm(x_ref[...], g_ref[...]).astype(BF16)

    def mixer_glu():
        res = _dot(h_s[...], wglu_s[...])
        for c in range(N_SLAB):
            val = res[:, 2 * c * LANES:(2 * c + 1) * LANES] + bin_ref[:, GA0 + c * LANES:GA0 + (c + 1) * LANES]
            gate = res[:, (2 * c + 1) * LANES:(2 * c + 2) * LANES] + bin_ref[:, GB0 + c * LANES:GB0 + (c + 1) * LANES]
            u_s[c, UPAD:UPAD + tq, :] = val * _sigmoid(gate)

    def mixer_conv(r, after):
        n_vreg = CONV_ROWS // 16
        for c in range(N_SLAB):
            lanes = slice(c * LANES, (c + 1) * LANES)
            base = r * CONV_ROWS + UPAD - (CONV_K - 1)
            gate = after
            even = odd = cb_ref[:, lanes] + jnp.tile(after, (n_vreg, 1))
            w_prev = None
            for t in range(CONV_K + 1):
                rows = u_s[c, pl.ds(base + t, CONV_ROWS // 2, stride=2), :]
                if t > 0:
                    odd = odd + w_prev * rows
                if t < CONV_K:
                    if t > 0 and t % TAP_GROUP == 0:
                        gate = _zero_after(even)
                    w_prev = jnp.tile(cw_ref[t:t + 1, lanes] + gate, (n_vreg, 1))
                    even = even + w_prev * rows
            y_s[c, pl.ds(r * CONV_ROWS, CONV_ROWS // 2, stride=2), :] = even
            y_s[c, pl.ds(r * CONV_ROWS + 1, CONV_ROWS // 2, stride=2), :] = odd

    def mixer_conv_act():
        y = jnp.concatenate([y_s[c] for c in range(N_SLAB)], axis=1)
        mu = jnp.mean(y, axis=-1, keepdims=True)
        yc = y - mu
        var = jnp.mean(yc * yc, axis=-1, keepdims=True)
        z = yc * lax.rsqrt(var + EPS) * lng_ref[...] + lnb_ref[...]
        cv_s[...] = (z * _sigmoid(z)).astype(BF16)

    def mixer_qkv():
        q_s[...] = proj(Q0, ATTN_W).astype(BF16)
        kv = jnp.concatenate(
            [_dot(h_s[r * (tq // 2):(r + 1) * (tq // 2), :], win_ref[:, K0:K0 + 2 * KV_W]) for r in range(2)],
            axis=0) + bin_ref[:, K0:K0 + 2 * KV_W]
        lo = lax.broadcasted_iota(jnp.int32, (tq, LANES), 1) < HEAD_DIM
        for src, dst in ((kv[:, :KV_W] * SCALE, kt_s), (kv[:, KV_W:], vt_s)):
            rolled = pltpu.roll(src, HEAD_DIM, axis=1)
            g0 = jnp.where(lo, src, rolled).astype(BF16)
            g1 = jnp.where(lo, rolled, src).astype(BF16)
            dst[0, BLOCK:BLOCK + tq, :] = jnp.concatenate([g0, g0], axis=1)
            dst[1, BLOCK:BLOCK + tq, :] = jnp.concatenate([g1, g1], axis=1)

    head_of_lane = lax.broadcasted_iota(jnp.int32, (BLOCK, GW), 1) // HEAD_DIM
    kmin_first = jnp.where(j == 0, BLOCK, 0)

    def mixer_scores(qb, g):
        valid = in_window & (kj >= kmin_first) if qb == 0 else in_window
        rows = slice(qb * BLOCK, (qb + 1) * BLOCK)
        q4 = q_s[rows, g * GW:(g + 1) * GW]
        lhs = jnp.concatenate(
            [jnp.where(head_of_lane == hh, q4, jnp.zeros_like(q4)) for hh in range(GROUP)], axis=0)
        keys = kt_s[g, qb * BLOCK:qb * BLOCK + 2 * BLOCK, :]
        s = lax.dot_general(lhs, keys, (((1,), (1,)), ((), ())), preferred_element_type=F32)
        ps, inv_ls = [], []
        for hh in range(GROUP):
            head = g * GROUP + hh
            sink = sinks_ref[head]
            sh = jnp.where(valid, s[hh * BLOCK:(hh + 1) * BLOCK, :] - alibi_s[head], NEG)
            m = jnp.maximum(jnp.max(sh, axis=-1, keepdims=True), sink)
            p = jnp.exp(sh - m)
            denom = jnp.sum(p, axis=-1, keepdims=True) + jnp.exp(sink - m)
            ps.append(p.astype(BF16))
            inv_ls.append(1.0 / denom)
        return qb, g, jnp.concatenate(ps, axis=0), inv_ls

    def mixer_attend(qb, g, pmat, inv_ls):
        rows = slice(qb * BLOCK, (qb + 1) * BLOCK)
        o = _dot(pmat, vt_s[g, qb * BLOCK:qb * BLOCK + 2 * BLOCK, :])
        out = jnp.zeros((BLOCK, GW), F32)
        for hh in range(GROUP):
            out = jnp.where(head_of_lane == hh, o[hh * BLOCK:(hh + 1) * BLOCK, :] * inv_ls[hh], out)
        attn_s[rows, g * GW:(g + 1) * GW] = out.astype(BF16)

    def mixer_merge(nc):
        half = D_MODEL // 2
        cols = slice(nc * half, (nc + 1) * half)
        gate_a = _sigmoid(proj(GTA0 + nc * half, half))
        gate_c = _sigmoid(proj(GTC0 + nc * half, half))
        br_c = _dot(cv_s[...], wcp_ref[:, cols]) + bcp_ref[:, cols]
        br_a = _dot(attn_s[...], wap_ref[:, cols])
        mg_s[:, cols] = (gate_a * br_a + gate_c * br_c).astype(BF16)

    def mixer_out():
        xm_s[...] = x_ref[...] + _dot(mg_s[...], wout_ref[...])
        kt_s[:, 0:BLOCK, :] = kt_s[:, tq:tq + BLOCK, :]
        vt_s[:, 0:BLOCK, :] = vt_s[:, tq:tq + BLOCK, :]
        u_s[:, 0:UPAD, :] = u_s[:, tq:tq + UPAD, :]

    def mlp_norm():
        h2_s[...] = _rms_norm(xm_s[...], g2_ref[...]).astype(BF16)

    def mlp_up(c):
        cols = slice(c * FF_CHUNK, (c + 1) * FF_CHUNK)
        a = jnp.maximum(_dot(h2_s[...], w1_ref[:, cols]), 0.0)
        return (a * a).astype(BF16)

    def mlp_down(c, a):
        part = _dot(a, w2_ref[c * FF_CHUNK:(c + 1) * FF_CHUNK, :])
        if c == 0:
            o_ref[...] = xm_s[...] + part
        else:
            o_ref[...] += part
        return _zero_after(part)

    n_ff = D_FF // FF_CHUNK
    mixer_norm()
    after = mlp_down(0, mlp_up(0))
    mixer_glu()
    conv_chunk = 0
    for c in range(1, 1 + tq // CONV_ROWS // 2):
        a = mlp_up(c)
        mixer_conv(conv_chunk, after)
        after = _zero_after(a)
        nxt = mlp_down(c, a)
        mixer_conv(conv_chunk + 1, after)
        after = nxt
        conv_chunk += 2
    mixer_qkv()
    mixer_conv_act()
    blocks = [(qb, g) for qb in range(tq // BLOCK) for g in range(N_KV)]
    pending = [mixer_scores(*blk) for blk in blocks[0:2]]
    mlp_chunks = []
    for c in range(1 + tq // CONV_ROWS // 2, n_ff - 1):
        mlp_chunks += [("up", c), ("down", c)]
    mlp_chunks.append(("up", n_ff - 1))
    a = None
    for k, (kind, c) in enumerate(mlp_chunks):
        if kind == "up":
            a = mlp_up(c)
        else:
            mlp_down(c, a)
        for scored in pending:
            mixer_attend(*scored)
        pending = [mixer_scores(*blk) for blk in blocks[2 * k + 2:2 * k + 4]]
    mlp_down(n_ff - 1, a)
    if final_norm:
        o_ref[...] = _rms_norm(o_ref[...], gf_ref[...])
    mixer_merge(0)
    mixer_merge(1)
    mixer_out()
    mlp_norm()


def _resident(shape):
    return pl.BlockSpec(shape, lambda *_: (0,) * len(shape), pipeline_mode=pl.Buffered(1))


def _layer(x2, tiles_per_seq, final_norm, sinks, g, w_in, b_in, cw, cb, lng, lnb, wap, wcp, bcp, wout, g2, w1, w2, gf):
    t, d = x2.shape
    n_tiles = t // TQ
    assert TQ // CONV_ROWS == 8 and D_FF // FF_CHUNK == 8 and (TQ // BLOCK) * N_KV == 8
    return pl.pallas_call(
        functools.partial(_layer_kernel, tiles_per_seq=tiles_per_seq, final_norm=final_norm),
        grid=(n_tiles + 1,),
        in_specs=[
            pl.BlockSpec(memory_space=pltpu.SMEM),
            pl.BlockSpec((TQ, d), lambda i: (jnp.minimum(i, n_tiles - 1), 0)),
            _resident((1, d)),
            _resident((d, IN_W)),
            _resident((1, IN_W)),
            _resident((CONV_K, CONV_C)),
            _resident((1, CONV_C)),
            _resident((1, CONV_C)),
            _resident((1, CONV_C)),
            _resident((ATTN_W, d)),
            _resident((CONV_C, d)),
            _resident((1, d)),
            _resident((d, d)),
            _resident((1, d)),
            _resident((d, D_FF)),
            _resident((D_FF, d)),
            _resident((1, d)),
        ],
        out_specs=pl.BlockSpec((TQ, d), lambda i: (jnp.maximum(i - 1, 0), 0)),
        out_shape=jax.ShapeDtypeStruct(x2.shape, x2.dtype),
        scratch_shapes=[
            pltpu.VMEM((TQ, d), BF16),
            pltpu.VMEM((TQ, ATTN_W), BF16),
            pltpu.VMEM((N_KV, BLOCK + TQ, GW), BF16),
            pltpu.VMEM((N_KV, BLOCK + TQ, GW), BF16),
            pltpu.VMEM((N_SLAB, UPAD + TQ, LANES), F32),
            pltpu.VMEM((N_SLAB, TQ, LANES), F32),
            pltpu.VMEM((TQ, ATTN_W), BF16),
            pltpu.VMEM((TQ, CONV_C), BF16),
            pltpu.VMEM((TQ, d), BF16),
            pltpu.VMEM((N_Q, BLOCK, 2 * BLOCK), F32),
            pltpu.VMEM((TQ, d), F32),
            pltpu.VMEM((TQ, d), BF16),
            pltpu.VMEM((d, 2 * CONV_C), BF16),
        ],
        compiler_params=pltpu.CompilerParams(dimension_semantics=("arbitrary",), vmem_limit_bytes=VMEM_LIMIT),
        name="layer",
    )(sinks, x2, g, w_in, b_in, cw, cb, lng, lnb, wap, wcp, bcp, wout, g2, w1, w2, gf)


def kernel(x, mix_norm_g, w_in, b_in, sinks, conv_w, conv_b, conv_ln_g, conv_ln_b, w_attn_proj, w_conv_proj,
           b_conv_proj, w_out, mlp_norm_g, w_mlp1, w_mlp2, final_norm_g):
    b, s, d = x.shape
    depth = w_in.shape[0]
    assert s % TQ == 0 and TQ % BLOCK == 0 and TQ % CONV_ROWS == 0
    row = lambda v: v.reshape(1, -1)
    x2 = x.reshape(b * s, d)
    for l in range(depth):
        x2 = _layer(x2, s // TQ, l == depth - 1, sinks[l], row(mix_norm_g[l]), w_in[l].astype(BF16), row(b_in[l]),
                    conv_w[l], row(conv_b[l]), row(conv_ln_g[l]), row(conv_ln_b[l]), w_attn_proj[l].astype(BF16),
                    w_conv_proj[l].astype(BF16), row(b_conv_proj[l]), w_out[l].astype(BF16), row(mlp_norm_g[l]),
                    w_mlp1[l].astype(BF16), w_mlp2[l].astype(BF16), row(final_norm_g))
    return x2.reshape(b, s, d)
```

```python
import functools
import math

import jax
import jax.numpy as jnp
from jax import lax
from jax.experimental import pallas as pl
from jax.experimental.pallas import tpu as pltpu

D_MODEL = 1024
N_Q = 8
N_KV = 2
GROUP = N_Q // N_KV
HEAD_DIM = 64
ATTN_W = N_Q * HEAD_DIM
KV_W = N_KV * HEAD_DIM
WINDOW = 128
BLOCK = 128
CONV_C = D_MODEL // 2
CONV_K = 31
D_FF = 4 * D_MODEL
IN_W = ATTN_W + 2 * KV_W + 2 * CONV_C + 2 * D_MODEL
EPS = 1e-6
NEG = -1e30
SCALE = 1.0 / math.sqrt(HEAD_DIM)
SLOPES = tuple(2.0 ** (-8.0 * h / N_Q) for h in range(1, N_Q + 1))

Q0 = 0
K0 = Q0 + ATTN_W
GA0 = K0 + 2 * KV_W
GB0 = GA0 + CONV_C
GTA0 = GB0 + CONV_C
GTC0 = GTA0 + D_MODEL

LANES = 128
GW = GROUP * HEAD_DIM
N_SLAB = CONV_C // LANES
UPAD = 32
TQ = 512
CONV_ROWS = 64
TAP_GROUP = 4
FF_CHUNK = 512
VMEM_LIMIT = 58 * 1024 * 1024

F32 = jnp.float32
BF16 = jnp.bfloat16


def _dot(a, b):
    return jnp.dot(a, b, preferred_element_type=F32)


def _sigmoid(x):
    return 0.5 * jnp.tanh(0.5 * x) + 0.5


def _rms_norm(x, g):
    ms = jnp.mean(x * x, axis=-1, keepdims=True)
    return x * lax.rsqrt(ms + EPS) * g


def _zero_after(v):
    rows = 32 // jnp.dtype(v.dtype).itemsize
    bits = pltpu.bitcast(v[-rows:, -LANES:], jnp.uint32)
    half = jnp.uint32(16)
    return lax.shift_right_logical(lax.shift_right_logical(bits, half), half).astype(F32)


def _layer_kernel(sinks_ref, x_ref, g_ref, win_ref, bin_ref, cw_ref, cb_ref, lng_ref, lnb_ref,
                  wap_ref, wcp_ref, bcp_ref, wout_ref, g2_ref, w1_ref, w2_ref, gf_ref, o_ref,
                  h_s, q_s, kt_s, vt_s, u_s, y_s, attn_s, cv_s, mg_s, alibi_s, xm_s, h2_s, wglu_s,
                  *, tiles_per_seq, final_norm):
    i = pl.program_id(0)
    j = lax.rem(i, tiles_per_seq)
    tq = x_ref.shape[0]

    qi = lax.broadcasted_iota(jnp.int32, (BLOCK, 2 * BLOCK), 0)
    kj = lax.broadcasted_iota(jnp.int32, (BLOCK, 2 * BLOCK), 1)
    dist = qi + BLOCK - kj
    in_window = (dist >= 0) & (dist < WINDOW)

    @pl.when(i == 0)
    def _():
        xm_s[...] = jnp.zeros(xm_s.shape, F32)
        h2_s[...] = jnp.zeros(h2_s.shape, BF16)
        distf = dist.astype(F32)
        for head in range(N_Q):
            alibi_s[head] = SLOPES[head] * distf
        for c in range(N_SLAB):
            wglu_s[:, 2 * c * LANES:(2 * c + 1) * LANES] = win_ref[:, GA0 + c * LANES:GA0 + (c + 1) * LANES]
            wglu_s[:, (2 * c + 1) * LANES:(2 * c + 2) * LANES] = win_ref[:, GB0 + c * LANES:GB0 + (c + 1) * LANES]

    @pl.when(j == 0)
    def _():
        kt_s[:, :, 0:BLOCK] = jnp.zeros((N_KV, GW, BLOCK), BF16)
        vt_s[:, 0:BLOCK, :] = jnp.zeros((N_KV, BLOCK, GW), BF16)
        u_s[:, 0:UPAD, :] = jnp.zeros((N_SLAB, UPAD, LANES), F32)

    def proj(c0, width):
        return _dot(h_s[...], win_ref[:, c0:c0 + width]) + bin_ref[:, c0:c0 + width]

    def mixer_norm():
        h_s[...] = _rms_norm(x_ref[...], g_ref[...]).astype(BF16)

    def mixer_glu():
        res = _dot(h_s[...], wglu_s[...])
        for c in range(N_SLAB):
            val = res[:, 2 * c * LANES:(2 * c + 1) * LANES] + bin_ref[:, GA0 + c * LANES:GA0 + (c + 1) * LANES]
            gate = res[:, (2 * c + 1) * LANES:(2 * c + 2) * LANES] + bin_ref[:, GB0 + c * LANES:GB0 + (c + 1) * LANES]
            u_s[c, UPAD:UPAD + tq, :] = val * _sigmoid(gate)

    def mixer_conv(r, after):
        n_vreg = CONV_ROWS // 16
        for c in range(N_SLAB):
            lanes = slice(c * LANES, (c + 1) * LANES)
            base = r * CONV_ROWS + UPAD - (CONV_K - 1)
            gate = after
            even = odd = cb_ref[:, lanes] + jnp.tile(after, (n_vreg, 1))
            w_prev = None
            for t in range(CONV_K + 1):
                rows = u_s[c, pl.ds(base + t, CONV_ROWS // 2, stride=2), :]
                if t > 0:
                    odd = odd + w_prev * rows
                if t < CONV_K:
                    if t > 0 and t % TAP_GROUP == 0:
                        gate = _zero_after(even)
                    w_prev = jnp.tile(cw_ref[t:t + 1, lanes] + gate, (n_vreg, 1))
                    even = even + w_prev * rows
            y_s[c, pl.ds(r * CONV_ROWS, CONV_ROWS // 2, stride=2), :] = even
            y_s[c, pl.ds(r * CONV_ROWS + 1, CONV_ROWS // 2, stride=2), :] = odd

    def mixer_conv_act():
        y = jnp.concatenate([y_s[c] for c in range(N_SLAB)], axis=1)
        mu = jnp.mean(y, axis=-1, keepdims=True)
        yc = y - mu
        var = jnp.mean(yc * yc, axis=-1, keepdims=True)
        z = yc * lax.rsqrt(var + EPS) * lng_ref[...] + lnb_ref[...]
        cv_s[...] = (z * _sigmoid(z)).astype(BF16)

    def mixer_qkv():
        q_s[...] = proj(Q0, ATTN_W).astype(BF16)
        kv = jnp.concatenate(
            [_dot(h_s[r * (tq // 2):(r + 1) * (tq // 2), :], win_ref[:, K0:K0 + 2 * KV_W]) for r in range(2)],
            axis=0) + bin_ref[:, K0:K0 + 2 * KV_W]
        k_t = (kv[:, :KV_W] * SCALE).T
        for g in range(N_KV):
            k_g = k_t[g * HEAD_DIM:(g + 1) * HEAD_DIM, :].astype(BF16)
            kt_s[g, :, BLOCK:BLOCK + tq] = jnp.concatenate([k_g] * GROUP, axis=0)
        v = kv[:, KV_W:]
        lo = lax.broadcasted_iota(jnp.int32, (tq, LANES), 1) < HEAD_DIM
        rolled = pltpu.roll(v, HEAD_DIM, axis=1)
        g0 = jnp.where(lo, v, rolled).astype(BF16)
        g1 = jnp.where(lo, rolled, v).astype(BF16)
        vt_s[0, BLOCK:BLOCK + tq, :] = jnp.concatenate([g0, g0], axis=1)
        vt_s[1, BLOCK:BLOCK + tq, :] = jnp.concatenate([g1, g1], axis=1)

    head_of_lane = lax.broadcasted_iota(jnp.int32, (BLOCK, GW), 1) // HEAD_DIM
    kmin_first = jnp.where(j == 0, BLOCK, 0)

    def mixer_scores(qb, g):
        valid = in_window & (kj >= kmin_first) if qb == 0 else in_window
        rows = slice(qb * BLOCK, (qb + 1) * BLOCK)
        q4 = q_s[rows, g * GW:(g + 1) * GW]
        lhs = jnp.concatenate(
            [jnp.where(head_of_lane == hh, q4, jnp.zeros_like(q4)) for hh in range(GROUP)], axis=0)
        s = _dot(lhs, kt_s[g, :, qb * BLOCK:qb * BLOCK + 2 * BLOCK])
        ps, inv_ls = [], []
        for hh in range(GROUP):
            head = g * GROUP + hh
            sink = sinks_ref[head]
            sh = jnp.where(valid, s[hh * BLOCK:(hh + 1) * BLOCK, :] - alibi_s[head], NEG)
            m = jnp.maximum(jnp.max(sh, axis=-1, keepdims=True), sink)
            p = jnp.exp(sh - m)
            denom = jnp.sum(p, axis=-1, keepdims=True) + jnp.exp(sink - m)
            ps.append(p.astype(BF16))
            inv_ls.append(1.0 / denom)
        return qb, g, jnp.concatenate(ps, axis=0), inv_ls

    def mixer_attend(qb, g, pmat, inv_ls):
        rows = slice(qb * BLOCK, (qb + 1) * BLOCK)
        o = _dot(pmat, vt_s[g, qb * BLOCK:qb * BLOCK + 2 * BLOCK, :])
        out = jnp.zeros((BLOCK, GW), F32)
        for hh in range(GROUP):
            out = jnp.where(head_of_lane == hh, o[hh * BLOCK:(hh + 1) * BLOCK, :] * inv_ls[hh], out)
        attn_s[rows, g * GW:(g + 1) * GW] = out.astype(BF16)

    def mixer_merge(nc):
        half = D_MODEL // 2
        cols = slice(nc * half, (nc + 1) * half)
        gate_a = _sigmoid(proj(GTA0 + nc * half, half))
        gate_c = _sigmoid(proj(GTC0 + nc * half, half))
        br_c = _dot(cv_s[...], wcp_ref[:, cols]) + bcp_ref[:, cols]
        br_a = _dot(attn_s[...], wap_ref[:, cols])
        mg_s[:, cols] = (gate_a * br_a + gate_c * br_c).astype(BF16)

    def mixer_out():
        xm_s[...] = x_ref[...] + _dot(mg_s[...], wout_ref[...])
        kt_s[:, :, 0:BLOCK] = kt_s[:, :, tq:tq + BLOCK]
        vt_s[:, 0:BLOCK, :] = vt_s[:, tq:tq + BLOCK, :]
        u_s[:, 0:UPAD, :] = u_s[:, tq:tq + UPAD, :]

    def mlp_norm():
        h2_s[...] = _rms_norm(xm_s[...], g2_ref[...]).astype(BF16)

    def mlp_up(c):
        cols = slice(c * FF_CHUNK, (c + 1) * FF_CHUNK)
        a = jnp.maximum(_dot(h2_s[...], w1_ref[:, cols]), 0.0)
        return (a * a).astype(BF16)

    def mlp_down(c, a):
        part = _dot(a, w2_ref[c * FF_CHUNK:(c + 1) * FF_CHUNK, :])
        if c == 0:
            o_ref[...] = xm_s[...] + part
        else:
            o_ref[...] += part
        return _zero_after(part)

    n_ff = D_FF // FF_CHUNK
    mixer_norm()
    after = mlp_down(0, mlp_up(0))
    mixer_glu()
    conv_chunk = 0
    for c in range(1, 1 + tq // CONV_ROWS // 2):
        a = mlp_up(c)
        mixer_conv(conv_chunk, after)
        after = _zero_after(a)
        nxt = mlp_down(c, a)
        mixer_conv(conv_chunk + 1, after)
        after = nxt
        conv_chunk += 2
    mixer_qkv()
    mixer_conv_act()
    blocks = [(qb, g) for qb in range(tq // BLOCK) for g in range(N_KV)]
    pending = [mixer_scores(*blk) for blk in blocks[0:2]]
    mlp_chunks = []
    for c in range(1 + tq // CONV_ROWS // 2, n_ff - 1):
        mlp_chunks += [("up", c), ("down", c)]
    mlp_chunks.append(("up", n_ff - 1))
    a = None
    for k, (kind, c) in enumerate(mlp_chunks):
        if kind == "up":
            a = mlp_up(c)
        else:
            mlp_down(c, a)
        for scored in pending:
            mixer_attend(*scored)
        pending = [mixer_scores(*blk) for blk in blocks[2 * k + 2:2 * k + 4]]
    mlp_down(n_ff - 1, a)
    if final_norm:
        o_ref[...] = _rms_norm(o_ref[...], gf_ref[...])
    mixer_merge(0)
    mixer_merge(1)
    mixer_out()
    mlp_norm()


def _resident(shape):
    return pl.BlockSpec(shape, lambda *_: (0,) * len(shape), pipeline_mode=pl.Buffered(1))


def _layer(x2, tiles_per_seq, final_norm, sinks, g, w_in, b_in, cw, cb, lng, lnb, wap, wcp, bcp, wout, g2, w1, w2, gf):
    t, d = x2.shape
    n_tiles = t // TQ
    assert TQ // CONV_ROWS == 8 and D_FF // FF_CHUNK == 8 and (TQ // BLOCK) * N_KV == 8
    return pl.pallas_call(
        functools.partial(_layer_kernel, tiles_per_seq=tiles_per_seq, final_norm=final_norm),
        grid=(n_tiles + 1,),
        in_specs=[
            pl.BlockSpec(memory_space=pltpu.SMEM),
            pl.BlockSpec((TQ, d), lambda i: (jnp.minimum(i, n_tiles - 1), 0)),
            _resident((1, d)),
            _resident((d, IN_W)),
            _resident((1, IN_W)),
            _resident((CONV_K, CONV_C)),
            _resident((1, CONV_C)),
            _resident((1, CONV_C)),
            _resident((1, CONV_C)),
            _resident((ATTN_W, d)),
            _resident((CONV_C, d)),
            _resident((1, d)),
            _resident((d, d)),
            _resident((1, d)),
            _resident((d, D_FF)),
            _resident((D_FF, d)),
            _resident((1, d)),
        ],
        out_specs=pl.BlockSpec((TQ, d), lambda i: (jnp.maximum(i - 1, 0), 0)),
        out_shape=jax.ShapeDtypeStruct(x2.shape, x2.dtype),
        scratch_shapes=[
            pltpu.VMEM((TQ, d), BF16),
            pltpu.VMEM((TQ, ATTN_W), BF16),
            pltpu.VMEM((N_KV, GW, BLOCK + TQ), BF16),
            pltpu.VMEM((N_KV, BLOCK + TQ, GW), BF16),
            pltpu.VMEM((N_SLAB, UPAD + TQ, LANES), F32),
            pltpu.VMEM((N_SLAB, TQ, LANES), F32),
            pltpu.VMEM((TQ, ATTN_W), BF16),
            pltpu.VMEM((TQ, CONV_C), BF16),
            pltpu.VMEM((TQ, d), BF16),
            pltpu.VMEM((N_Q, BLOCK, 2 * BLOCK), F32),
            pltpu.VMEM((TQ, d), F32),
            pltpu.VMEM((TQ, d), BF16),
            pltpu.VMEM((d, 2 * CONV_C), BF16),
        ],
        compiler_params=pltpu.CompilerParams(dimension_semantics=("arbitrary",), vmem_limit_bytes=VMEM_LIMIT),
        name="layer",
    )(sinks, x2, g, w_in, b_in, cw, cb, lng, lnb, wap, wcp, bcp, wout, g2, w1, w2, gf)


def kernel(x, mix_norm_g, w_in, b_in, sinks, conv_w, conv_b, conv_ln_g, conv_ln_b, w_attn_proj, w_conv_proj,
           b_conv_proj, w_out, mlp_norm_g, w_mlp1, w_mlp2, final_norm_g):
    b, s, d = x.shape
    depth = w_in.shape[0]
    assert s % TQ == 0 and TQ % BLOCK == 0 and TQ % CONV_ROWS == 0
    row = lambda v: v.reshape(1, -1)
    x2 = x.reshape(b * s, d)
    for l in range(depth):
        x2 = _layer(x2, s // TQ, l == depth - 1, sinks[l], row(mix_norm_g[l]), w_in[l].astype(BF16), row(b_in[l]),
                    conv_w[l], row(conv_b[l]), row(conv_ln_g[l]), row(conv_ln_b[l]), w_attn_proj[l].astype(BF16),
                    w_conv_proj[l].astype(BF16), row(b_conv_proj[l]), w_out[l].astype(BF16), row(mlp_norm_g[l]),
                    w_mlp1[l].astype(BF16), w_mlp2[l].astype(BF16), row(final_norm_g))
    return x2.reshape(b, s, d)
```

```python
import functools
import math

import jax
import jax.numpy as jnp
from jax import lax
from jax.experimental import pallas as pl
from jax.experimental.pallas import tpu as pltpu

D_MODEL = 1024
N_Q = 8
N_KV = 2
GROUP = N_Q // N_KV
HEAD_DIM = 64
ATTN_W = N_Q * HEAD_DIM
KV_W = N_KV * HEAD_DIM
WINDOW = 128
BLOCK = 128
CONV_C = D_MODEL // 2
CONV_K = 31
D_FF = 4 * D_MODEL
IN_W = ATTN_W + 2 * KV_W + 2 * CONV_C + 2 * D_MODEL
EPS = 1e-6
NEG = -1e30
SCALE = 1.0 / math.sqrt(HEAD_DIM)
SLOPES = tuple(2.0 ** (-8.0 * h / N_Q) for h in range(1, N_Q + 1))

Q0 = 0
K0 = Q0 + ATTN_W
GA0 = K0 + 2 * KV_W
GB0 = GA0 + CONV_C
GTA0 = GB0 + CONV_C
GTC0 = GTA0 + D_MODEL

LANES = 128
GW = GROUP * HEAD_DIM
N_SLAB = CONV_C // LANES
UPAD = 32
TQ = 512
CONV_ROWS = 64
TAP_GROUP = 4
FF_CHUNK = 512
VMEM_LIMIT = 58 * 1024 * 1024

F32 = jnp.float32
BF16 = jnp.bfloat16


def _dot(a, b):
    return jnp.dot(a, b, preferred_element_type=F32)


def _sigmoid(x):
    return 0.5 * jnp.tanh(0.5 * x) + 0.5


def _rms_norm(x, g):
    ms = jnp.mean(x * x, axis=-1, keepdims=True)
    return x * lax.rsqrt(ms + EPS) * g


def _zero_after(v):
    rows = 32 // jnp.dtype(v.dtype).itemsize
    bits = pltpu.bitcast(v[-rows:, -LANES:], jnp.uint32)
    half = jnp.uint32(16)
    return lax.shift_right_logical(lax.shift_right_logical(bits, half), half).astype(F32)


def _layer_kernel(sinks_ref, x_ref, g_ref, win_ref, bin_ref, cw_ref, cb_ref, lng_ref, lnb_ref,
                  wap_ref, wcp_ref, bcp_ref, wout_ref, g2_ref, w1_ref, w2_ref, gf_ref, o_ref,
                  h_s, q_s, kt_s, vt_s, u_s, y_s, attn_s, cv_s, mg_s, alibi_s, xm_s, h2_s, wglu_s,
                  *, tiles_per_seq, final_norm):
    i = pl.program_id(0)
    j = lax.rem(i, tiles_per_seq)
    tq = x_ref.shape[0]

    qi = lax.broadcasted_iota(jnp.int32, (BLOCK, 2 * BLOCK), 0)
    kj = lax.broadcasted_iota(jnp.int32, (BLOCK, 2 * BLOCK), 1)
    dist = qi + BLOCK - kj
    in_window = (dist >= 0) & (dist < WINDOW)

    @pl.when(i == 0)
    def _():
        xm_s[...] = jnp.zeros(xm_s.shape, F32)
        h2_s[...] = jnp.zeros(h2_s.shape, BF16)
        distf = dist.astype(F32)
        for head in range(N_Q):
            alibi_s[head] = SLOPES[head] * distf
        for c in range(N_SLAB):
            wglu_s[:, 2 * c * LANES:(2 * c + 1) * LANES] = win_ref[:, GA0 + c * LANES:GA0 + (c + 1) * LANES]
            wglu_s[:, (2 * c + 1) * LANES:(2 * c + 2) * LANES] = win_ref[:, GB0 + c * LANES:GB0 + (c + 1) * LANES]

    @pl.when(j == 0)
    def _():
        kt_s[:, 0:BLOCK, :] = jnp.zeros((N_KV, BLOCK, GW), BF16)
        vt_s[:, 0:BLOCK, :] = jnp.zeros((N_KV, BLOCK, GW), BF16)
        u_s[:, 0:UPAD, :] = jnp.zeros((N_SLAB, UPAD, LANES), F32)

    def proj(c0, width):
        return _dot(h_s[...], win_ref[:, c0:c0 + width]) + bin_ref[:, c0:c0 + width]

    def mixer_norm():
        h_s[...] = _rms_norm(x_ref[...], g_ref[...]).astype(BF16)

    def mixer_glu():
        res = _dot(h_s[...], wglu_s[...])
        for c in range(N_SLAB):
            val = res[:, 2 * c * LANES:(2 * c + 1) * LANES] + bin_ref[:, GA0 + c * LANES:GA0 + (c + 1) * LANES]
            gate = res[:, (2 * c + 1) * LANES:(2 * c + 2) * LANES] + bin_ref[:, GB0 + c * LANES:GB0 + (c + 1) * LANES]
            u_s[c, UPAD:UPAD + tq, :] = val * _sigmoid(gate)

    def mixer_conv(r, after):
        n_vreg = CONV_ROWS // 16
        for c in range(N_SLAB):
            lanes = slice(c * LANES, (c + 1) * LANES)
            base = r * CONV_ROWS + UPAD - (CONV_K - 1)
            gate = after
            even = odd = cb_ref[:, lanes] + jnp.tile(after, (n_vreg, 1))
            w_prev = None
            for t in range(CONV_K + 1):
                rows = u_s[c, pl.ds(base + t, CONV_ROWS // 2, stride=2), :]
                if t > 0:
                    odd = odd + w_prev * rows
                if t < CONV_K:
                    if t > 0 and t % TAP_GROUP == 0:
                        gate = _zero_after(even)
                    w_prev = jnp.tile(cw_ref[t:t + 1, lanes] + gate, (n_vreg, 1))
                    even = even + w_prev * rows
            y_s[c, pl.ds(r * CONV_ROWS, CONV_ROWS // 2, stride=2), :] = even
            y_s[c, pl.ds(r * CONV_ROWS + 1, CONV_ROWS // 2, stride=2), :] = odd

    def mixer_conv_act():
        y = jnp.concatenate([y_s[c] for c in range(N_SLAB)], axis=1)
        mu = jnp.mean(y, axis=-1, keepdims=True)
        yc = y - mu
        var = jnp.mean(yc * yc, axis=-1, keepdims=True)
        z = yc * lax.rsqrt(var + EPS) * lng_ref[...] + lnb_ref[...]
        cv_s[...] = (z * _sigmoid(z)).astype(BF16)

    def mixer_qkv():
        qkv = proj(Q0, ATTN_W + 2 * KV_W)
        q_s[...] = qkv[:, :ATTN_W].astype(BF16)
        kv = qkv[:, ATTN_W:]
        lo = lax.broadcasted_iota(jnp.int32, (tq, LANES), 1) < HEAD_DIM
        for src, dst in ((kv[:, :KV_W] * SCALE, kt_s), (kv[:, KV_W:], vt_s)):
            rolled = pltpu.roll(src, HEAD_DIM, axis=1)
            g0 = jnp.where(lo, src, rolled).astype(BF16)
            g1 = jnp.where(lo, rolled, src).astype(BF16)
            dst[0, BLOCK:BLOCK + tq, :] = jnp.concatenate([g0, g0], axis=1)
            dst[1, BLOCK:BLOCK + tq, :] = jnp.concatenate([g1, g1], axis=1)

    head_of_lane = lax.broadcasted_iota(jnp.int32, (BLOCK, GW), 1) // HEAD_DIM
    kmin_first = jnp.where(j == 0, BLOCK, 0)

    def mixer_scores(qb, g):
        valid = in_window & (kj >= kmin_first) if qb == 0 else in_window
        rows = slice(qb * BLOCK, (qb + 1) * BLOCK)
        q4 = q_s[rows, g * GW:(g + 1) * GW]
        lhs = jnp.concatenate(
            [jnp.where(head_of_lane == hh, q4, jnp.zeros_like(q4)) for hh in range(GROUP)], axis=0)
        keys = kt_s[g, qb * BLOCK:qb * BLOCK + 2 * BLOCK, :]
        s = lax.dot_general(lhs, keys, (((1,), (1,)), ((), ())), preferred_element_type=F32)
        ps, inv_ls = [], []
        for hh in range(GROUP):
            head = g * GROUP + hh
            sink = sinks_ref[head]
            sh = jnp.where(valid, s[hh * BLOCK:(hh + 1) * BLOCK, :] - alibi_s[head], NEG)
            m = jnp.maximum(jnp.max(sh, axis=-1, keepdims=True), sink)
            p = jnp.exp(sh - m)
            denom = jnp.sum(p, axis=-1, keepdims=True) + jnp.exp(sink - m)
            ps.append(p.astype(BF16))
            inv_ls.append(1.0 / denom)
        return qb, g, jnp.concatenate(ps, axis=0), inv_ls

    def mixer_attend(qb, g, pmat, inv_ls):
        rows = slice(qb * BLOCK, (qb + 1) * BLOCK)
        o = _dot(pmat, vt_s[g, qb * BLOCK:qb * BLOCK + 2 * BLOCK, :])
        out = jnp.zeros((BLOCK, GW), F32)
        for hh in range(GROUP):
            out = jnp.where(head_of_lane == hh, o[hh * BLOCK:(hh + 1) * BLOCK, :] * inv_ls[hh], out)
        attn_s[rows, g * GW:(g + 1) * GW] = out.astype(BF16)

    def mixer_merge(nc):
        half = D_MODEL // 2
        cols = slice(nc * half, (nc + 1) * half)
        gate_a = _sigmoid(proj(GTA0 + nc * half, half))
        gate_c = _sigmoid(proj(GTC0 + nc * half, half))
        br_c = _dot(cv_s[...], wcp_ref[:, cols]) + bcp_ref[:, cols]
        br_a = _dot(attn_s[...], wap_ref[:, cols])
        mg_s[:, cols] = (gate_a * br_a + gate_c * br_c).astype(BF16)

    def mixer_out():
        xm_s[...] = x_ref[...] + _dot(mg_s[...], wout_ref[...])
        kt_s[:, 0:BLOCK, :] = kt_s[:, tq:tq + BLOCK, :]
        vt_s[:, 0:BLOCK, :] = vt_s[:, tq:tq + BLOCK, :]
        u_s[:, 0:UPAD, :] = u_s[:, tq:tq + UPAD, :]

    def mlp_norm():
        h2_s[...] = _rms_norm(xm_s[...], g2_ref[...]).astype(BF16)

    def mlp_up(c):
        cols = slice(c * FF_CHUNK, (c + 1) * FF_CHUNK)
        a = jnp.maximum(_dot(h2_s[...], w1_ref[:, cols]), 0.0)
        return (a * a).astype(BF16)

    def mlp_down(c, a):
        part = _dot(a, w2_ref[c * FF_CHUNK:(c + 1) * FF_CHUNK, :])
        if c == 0:
            o_ref[...] = xm_s[...] + part
        else:
            o_ref[...] += part
        return _zero_after(part)

    n_ff = D_FF // FF_CHUNK
    mixer_norm()
    after = mlp_down(0, mlp_up(0))
    mixer_glu()
    conv_chunk = 0
    for c in range(1, 1 + tq // CONV_ROWS // 2):
        a = mlp_up(c)
        mixer_conv(conv_chunk, after)
        after = _zero_after(a)
        nxt = mlp_down(c, a)
        mixer_conv(conv_chunk + 1, after)
        after = nxt
        conv_chunk += 2
    mixer_qkv()
    mixer_conv_act()
    blocks = [(qb, g) for qb in range(tq // BLOCK) for g in range(N_KV)]
    pending = [mixer_scores(*blk) for blk in blocks[0:2]]
    mlp_chunks = []
    for c in range(1 + tq // CONV_ROWS // 2, n_ff - 1):
        mlp_chunks += [("up", c), ("down", c)]
    mlp_chunks.append(("up", n_ff - 1))
    a = None
    for k, (kind, c) in enumerate(mlp_chunks):
        if kind == "up":
            a = mlp_up(c)
        else:
            mlp_down(c, a)
        for scored in pending:
            mixer_attend(*scored)
        pending = [mixer_scores(*blk) for blk in blocks[2 * k + 2:2 * k + 4]]
    a_last = a
    mixer_merge(0)
    mixer_merge(1)
    mixer_out()
    mlp_down(n_ff - 1, a_last)
    mlp_norm()
    if final_norm:
        o_ref[...] = _rms_norm(o_ref[...], gf_ref[...])


def _resident(shape):
    return pl.BlockSpec(shape, lambda *_: (0,) * len(shape), pipeline_mode=pl.Buffered(1))


def _resident_layer(shape, layer):
    return pl.BlockSpec((None,) + shape, lambda *_: (layer,) + (0,) * len(shape), pipeline_mode=pl.Buffered(1))


def _layer(x2, tiles_per_seq, final_norm, layer, sinks, g, w_in, b_in, cw, cb, lng, lnb, wap, wcp, bcp, wout, g2, w1, w2, gf):
    t, d = x2.shape
    n_tiles = t // TQ
    assert TQ // CONV_ROWS == 8 and D_FF // FF_CHUNK == 8 and (TQ // BLOCK) * N_KV == 8
    return pl.pallas_call(
        functools.partial(_layer_kernel, tiles_per_seq=tiles_per_seq, final_norm=final_norm),
        grid=(n_tiles + 1,),
        in_specs=[
            pl.BlockSpec(memory_space=pltpu.SMEM),
            pl.BlockSpec((TQ, d), lambda i: (jnp.minimum(i, n_tiles - 1), 0)),
            _resident((1, d)),
            _resident_layer((d, IN_W), layer),
            _resident((1, IN_W)),
            _resident((CONV_K, CONV_C)),
            _resident((1, CONV_C)),
            _resident((1, CONV_C)),
            _resident((1, CONV_C)),
            _resident_layer((ATTN_W, d), layer),
            _resident_layer((CONV_C, d), layer),
            _resident((1, d)),
            _resident_layer((d, d), layer),
            _resident((1, d)),
            _resident_layer((d, D_FF), layer),
            _resident_layer((D_FF, d), layer),
            _resident((1, d)),
        ],
        out_specs=pl.BlockSpec((TQ, d), lambda i: (jnp.maximum(i - 1, 0), 0)),
        out_shape=jax.ShapeDtypeStruct(x2.shape, x2.dtype),
        scratch_shapes=[
            pltpu.VMEM((TQ, d), BF16),
            pltpu.VMEM((TQ, ATTN_W), BF16),
            pltpu.VMEM((N_KV, BLOCK + TQ, GW), BF16),
            pltpu.VMEM((N_KV, BLOCK + TQ, GW), BF16),
            pltpu.VMEM((N_SLAB, UPAD + TQ, LANES), F32),
            pltpu.VMEM((N_SLAB, TQ, LANES), F32),
            pltpu.VMEM((TQ, ATTN_W), BF16),
            pltpu.VMEM((TQ, CONV_C), BF16),
            pltpu.VMEM((TQ, d), BF16),
            pltpu.VMEM((N_Q, BLOCK, 2 * BLOCK), F32),
            pltpu.VMEM((TQ, d), F32),
            pltpu.VMEM((TQ, d), BF16),
            pltpu.VMEM((d, 2 * CONV_C), BF16),
        ],
        compiler_params=pltpu.CompilerParams(dimension_semantics=("arbitrary",), vmem_limit_bytes=VMEM_LIMIT),
        name="layer",
    )(sinks, x2, g, w_in, b_in, cw, cb, lng, lnb, wap, wcp, bcp, wout, g2, w1, w2, gf)


def kernel(x, mix_norm_g, w_in, b_in, sinks, conv_w, conv_b, conv_ln_g, conv_ln_b, w_attn_proj, w_conv_proj,
           b_conv_proj, w_out, mlp_norm_g, w_mlp1, w_mlp2, final_norm_g):
    b, s, d = x.shape
    depth = w_in.shape[0]
    assert s % TQ == 0 and TQ % BLOCK == 0 and TQ % CONV_ROWS == 0
    row = lambda v: v.reshape(1, -1)
    x2 = x.reshape(b * s, d)
    w_in, w_attn_proj, w_conv_proj, w_out, w_mlp1, w_mlp2 = (
        w.astype(BF16) for w in (w_in, w_attn_proj, w_conv_proj, w_out, w_mlp1, w_mlp2))
    for l in range(depth):
        x2 = _layer(x2, s // TQ, l == depth - 1, l, sinks[l], row(mix_norm_g[l]), w_in, row(b_in[l]),
                    conv_w[l], row(conv_b[l]), row(conv_ln_g[l]), row(conv_ln_b[l]), w_attn_proj,
                    w_conv_proj, row(b_conv_proj[l]), w_out, row(mlp_norm_g[l]),
                    w_mlp1, w_mlp2, row(final_norm_g))
    return x2.reshape(b, s, d)
```

```python
import functools
import math

import jax
import jax.numpy as jnp
from jax import lax
from jax.experimental import pallas as pl
from jax.experimental.pallas import tpu as pltpu

D_MODEL = 1024
N_Q = 8
N_KV = 2
GROUP = N_Q // N_KV
HEAD_DIM = 64
ATTN_W = N_Q * HEAD_DIM
KV_W = N_KV * HEAD_DIM
WINDOW = 128
BLOCK = 128
CONV_C = D_MODEL // 2
CONV_K = 31
D_FF = 4 * D_MODEL
IN_W = ATTN_W + 2 * KV_W + 2 * CONV_C + 2 * D_MODEL
EPS = 1e-6
NEG = -1e30
SCALE = 1.0 / math.sqrt(HEAD_DIM)
SLOPES = tuple(2.0 ** (-8.0 * h / N_Q) for h in range(1, N_Q + 1))

Q0 = 0
K0 = Q0 + ATTN_W
GA0 = K0 + 2 * KV_W
GB0 = GA0 + CONV_C
GTA0 = GB0 + CONV_C
GTC0 = GTA0 + D_MODEL

LANES = 128
GW = GROUP * HEAD_DIM
N_SLAB = CONV_C // LANES
UPAD = 32
TQ = 512
CONV_ROWS = 64
TAP_GROUP = 4
FF_CHUNK = 512
VMEM_LIMIT = 58 * 1024 * 1024

F32 = jnp.float32
BF16 = jnp.bfloat16


def _dot(a, b):
    return jnp.dot(a, b, preferred_element_type=F32)


def _sigmoid(x):
    return 0.5 * jnp.tanh(0.5 * x) + 0.5


def _rms_norm(x, g):
    ms = jnp.mean(x * x, axis=-1, keepdims=True)
    return x * lax.rsqrt(ms + EPS) * g


def _zero_after(v):
    rows = 32 // jnp.dtype(v.dtype).itemsize
    bits = pltpu.bitcast(v[-rows:, -LANES:], jnp.uint32)
    half = jnp.uint32(16)
    return lax.shift_right_logical(lax.shift_right_logical(bits, half), half).astype(F32)


def _layer_kernel(sinks_ref, x_ref, g_ref, win_ref, bin_ref, cw_ref, cb_ref, lng_ref, lnb_ref,
                  wap_ref, wcp_ref, bcp_ref, wout_ref, g2_ref, w1_ref, w2_ref, gf_ref, o_ref,
                  h_s, q_s, kt_s, vt_s, u_s, y_s, attn_s, cv_s, mg_s, alibi_s, xm_s, h2_s, wglu_s,
                  *, tiles_per_seq, final_norm):
    i = pl.program_id(0)
    j = lax.rem(i, tiles_per_seq)
    tq = x_ref.shape[0]

    qi = lax.broadcasted_iota(jnp.int32, (BLOCK, 2 * BLOCK), 0)
    kj = lax.broadcasted_iota(jnp.int32, (BLOCK, 2 * BLOCK), 1)
    dist = qi + BLOCK - kj
    in_window = (dist >= 0) & (dist < WINDOW)

    @pl.when(i == 0)
    def _():
        xm_s[...] = jnp.zeros(xm_s.shape, F32)
        h2_s[...] = jnp.zeros(h2_s.shape, BF16)
        distf = dist.astype(F32)
        for head in range(N_Q):
            alibi_s[head] = SLOPES[head] * distf
        for c in range(N_SLAB):
            wglu_s[:, 2 * c * LANES:(2 * c + 1) * LANES] = win_ref[:, GA0 + c * LANES:GA0 + (c + 1) * LANES]
            wglu_s[:, (2 * c + 1) * LANES:(2 * c + 2) * LANES] = win_ref[:, GB0 + c * LANES:GB0 + (c + 1) * LANES]

    @pl.when(j == 0)
    def _():
        kt_s[:, 0:BLOCK, :] = jnp.zeros((N_KV, BLOCK, GW), BF16)
        vt_s[:, 0:BLOCK, :] = jnp.zeros((N_KV, BLOCK, GW), BF16)
        u_s[:, 0:UPAD, :] = jnp.zeros((N_SLAB, UPAD, LANES), F32)

    def proj(c0, width):
        return _dot(h_s[...], win_ref[:, c0:c0 + width]) + bin_ref[:, c0:c0 + width]

    def mixer_norm():
        h_s[...] = _rms_norm(x_ref[...], g_ref[...]).astype(BF16)

    def mixer_glu():
        res = _dot(h_s[...], wglu_s[...])
        for c in range(N_SLAB):
            val = res[:, 2 * c * LANES:(2 * c + 1) * LANES] + bin_ref[:, GA0 + c * LANES:GA0 + (c + 1) * LANES]
            gate = res[:, (2 * c + 1) * LANES:(2 * c + 2) * LANES] + bin_ref[:, GB0 + c * LANES:GB0 + (c + 1) * LANES]
            u_s[c, UPAD:UPAD + tq, :] = val * _sigmoid(gate)

    def mixer_conv(r, after):
        n_vreg = CONV_ROWS // 16
        for c in range(N_SLAB):
            lanes = slice(c * LANES, (c + 1) * LANES)
            base = r * CONV_ROWS + UPAD - (CONV_K - 1)
            gate = after
            even = odd = cb_ref[:, lanes] + jnp.tile(after, (n_vreg, 1))
            w_prev = None
            for t in range(CONV_K + 1):
                rows = u_s[c, pl.ds(base + t, CONV_ROWS // 2, stride=2), :]
                if t > 0:
                    odd = odd + w_prev * rows
                if t < CONV_K:
                    if t > 0 and t % TAP_GROUP == 0:
                        gate = _zero_after(even)
                    w_prev = jnp.tile(cw_ref[t:t + 1, lanes] + gate, (n_vreg, 1))
                    even = even + w_prev * rows
            y_s[c, pl.ds(r * CONV_ROWS, CONV_ROWS // 2, stride=2), :] = even
            y_s[c, pl.ds(r * CONV_ROWS + 1, CONV_ROWS // 2, stride=2), :] = odd

    def mixer_conv_act():
        y = jnp.concatenate([y_s[c] for c in range(N_SLAB)], axis=1)
        mu = jnp.mean(y, axis=-1, keepdims=True)
        yc = y - mu
        var = jnp.mean(yc * yc, axis=-1, keepdims=True)
        z = yc * lax.rsqrt(var + EPS) * lng_ref[...] + lnb_ref[...]
        cv_s[...] = (z * _sigmoid(z)).astype(BF16)

    def mixer_qkv():
        qkv = proj(Q0, ATTN_W + 2 * KV_W)
        q_s[...] = qkv[:, :ATTN_W].astype(BF16)
        kv = qkv[:, ATTN_W:]
        lo = lax.broadcasted_iota(jnp.int32, (tq, LANES), 1) < HEAD_DIM
        for src, dst in ((kv[:, :KV_W] * SCALE, kt_s), (kv[:, KV_W:], vt_s)):
            rolled = pltpu.roll(src, HEAD_DIM, axis=1)
            g0 = jnp.where(lo, src, rolled).astype(BF16)
            g1 = jnp.where(lo, rolled, src).astype(BF16)
            dst[0, BLOCK:BLOCK + tq, :] = jnp.concatenate([g0, g0], axis=1)
            dst[1, BLOCK:BLOCK + tq, :] = jnp.concatenate([g1, g1], axis=1)

    head_of_lane = lax.broadcasted_iota(jnp.int32, (BLOCK, GW), 1) // HEAD_DIM
    kmin_first = jnp.where(j == 0, BLOCK, 0)

    def mixer_scores(qb, g):
        valid = in_window & (kj >= kmin_first) if qb == 0 else in_window
        rows = slice(qb * BLOCK, (qb + 1) * BLOCK)
        q4 = q_s[rows, g * GW:(g + 1) * GW]
        lhs = jnp.concatenate(
            [jnp.where(head_of_lane == hh, q4, jnp.zeros_like(q4)) for hh in range(GROUP)], axis=0)
        keys = kt_s[g, qb * BLOCK:qb * BLOCK + 2 * BLOCK, :]
        s = lax.dot_general(lhs, keys, (((1,), (1,)), ((), ())), preferred_element_type=F32)
        ps, inv_ls = [], []
        for hh in range(GROUP):
            head = g * GROUP + hh
            sink = sinks_ref[head]
            sh = jnp.where(valid, s[hh * BLOCK:(hh + 1) * BLOCK, :] - alibi_s[head], NEG)
            m = jnp.maximum(jnp.max(sh, axis=-1, keepdims=True), sink)
            p = jnp.exp(sh - m)
            denom = jnp.sum(p, axis=-1, keepdims=True) + jnp.exp(sink - m)
            ps.append(p.astype(BF16))
            inv_ls.append(1.0 / denom)
        return qb, g, jnp.concatenate(ps, axis=0), inv_ls

    def mixer_attend(qb, g, pmat, inv_ls):
        rows = slice(qb * BLOCK, (qb + 1) * BLOCK)
        o = _dot(pmat, vt_s[g, qb * BLOCK:qb * BLOCK + 2 * BLOCK, :])
        out = jnp.zeros((BLOCK, GW), F32)
        for hh in range(GROUP):
            out = jnp.where(head_of_lane == hh, o[hh * BLOCK:(hh + 1) * BLOCK, :] * inv_ls[hh], out)
        attn_s[rows, g * GW:(g + 1) * GW] = out.astype(BF16)

    def mixer_merge(nc):
        half = D_MODEL // 2
        cols = slice(nc * half, (nc + 1) * half)
        gate_a = _sigmoid(proj(GTA0 + nc * half, half))
        gate_c = _sigmoid(proj(GTC0 + nc * half, half))
        br_c = _dot(cv_s[...], wcp_ref[:, cols]) + bcp_ref[:, cols]
        br_a = _dot(attn_s[...], wap_ref[:, cols])
        mg_s[:, cols] = (gate_a * br_a + gate_c * br_c).astype(BF16)

    def mixer_out():
        xm_s[...] = x_ref[...] + _dot(mg_s[...], wout_ref[...])
        kt_s[:, 0:BLOCK, :] = kt_s[:, tq:tq + BLOCK, :]
        vt_s[:, 0:BLOCK, :] = vt_s[:, tq:tq + BLOCK, :]
        u_s[:, 0:UPAD, :] = u_s[:, tq:tq + UPAD, :]

    def mlp_norm():
        h2_s[...] = _rms_norm(xm_s[...], g2_ref[...]).astype(BF16)

    def mlp_up(c):
        cols = slice(c * FF_CHUNK, (c + 1) * FF_CHUNK)
        a = jnp.maximum(_dot(h2_s[...], w1_ref[:, cols]), 0.0)
        return (a * a).astype(BF16)

    def mlp_down(c, a):
        part = _dot(a, w2_ref[c * FF_CHUNK:(c + 1) * FF_CHUNK, :])
        if c == 0:
            o_ref[...] = xm_s[...] + part
        else:
            o_ref[...] += part
        return _zero_after(part)

    n_ff = D_FF // FF_CHUNK
    mixer_norm()
    after = mlp_down(0, mlp_up(0))
    mixer_glu()
    mixer_qkv()
    conv_chunk = 0
    for c in range(1, 1 + tq // CONV_ROWS // 2):
        a = mlp_up(c)
        mixer_conv(conv_chunk, after)
        after = _zero_after(a)
        nxt = mlp_down(c, a)
        mixer_conv(conv_chunk + 1, after)
        after = nxt
        conv_chunk += 2
    mixer_conv_act()
    blocks = [(qb, g) for qb in range(tq // BLOCK) for g in range(N_KV)]
    pending = [mixer_scores(*blk) for blk in blocks[0:2]]
    mlp_chunks = []
    for c in range(1 + tq // CONV_ROWS // 2, n_ff - 1):
        mlp_chunks += [("up", c), ("down", c)]
    mlp_chunks.append(("up", n_ff - 1))
    a = None
    for k, (kind, c) in enumerate(mlp_chunks):
        if kind == "up":
            a = mlp_up(c)
        else:
            mlp_down(c, a)
        for scored in pending:
            mixer_attend(*scored)
        pending = [mixer_scores(*blk) for blk in blocks[2 * k + 2:2 * k + 4]]
    a_last = a
    mixer_merge(0)
    mixer_merge(1)
    mixer_out()
    mlp_down(n_ff - 1, a_last)
    mlp_norm()
    if final_norm:
        o_ref[...] = _rms_norm(o_ref[...], gf_ref[...])


def _resident(shape):
    return pl.BlockSpec(shape, lambda *_: (0,) * len(shape), pipeline_mode=pl.Buffered(1))


def _layer(x2, tiles_per_seq, final_norm, sinks, g, w_in, b_in, cw, cb, lng, lnb, wap, wcp, bcp, wout, g2, w1, w2, gf):
    t, d = x2.shape
    n_tiles = t // TQ
    assert TQ // CONV_ROWS == 8 and D_FF // FF_CHUNK == 8 and (TQ // BLOCK) * N_KV == 8
    return pl.pallas_call(
        functools.partial(_layer_kernel, tiles_per_seq=tiles_per_seq, final_norm=final_norm),
        grid=(n_tiles + 1,),
        in_specs=[
            pl.BlockSpec(memory_space=pltpu.SMEM),
            pl.BlockSpec((TQ, d), lambda i: (jnp.minimum(i, n_tiles - 1), 0)),
            _resident((1, d)),
            _resident((d, IN_W)),
            _resident((1, IN_W)),
            _resident((CONV_K, CONV_C)),
            _resident((1, CONV_C)),
            _resident((1, CONV_C)),
            _resident((1, CONV_C)),
            _resident((ATTN_W, d)),
            _resident((CONV_C, d)),
            _resident((1, d)),
            _resident((d, d)),
            _resident((1, d)),
            _resident((d, D_FF)),
            _resident((D_FF, d)),
            _resident((1, d)),
        ],
        out_specs=pl.BlockSpec((TQ, d), lambda i: (jnp.maximum(i - 1, 0), 0)),
        out_shape=jax.ShapeDtypeStruct(x2.shape, x2.dtype),
        scratch_shapes=[
            pltpu.VMEM((TQ, d), BF16),
            pltpu.VMEM((TQ, ATTN_W), BF16),
            pltpu.VMEM((N_KV, BLOCK + TQ, GW), BF16),
            pltpu.VMEM((N_KV, BLOCK + TQ, GW), BF16),
            pltpu.VMEM((N_SLAB, UPAD + TQ, LANES), F32),
            pltpu.VMEM((N_SLAB, TQ, LANES), F32),
            pltpu.VMEM((TQ, ATTN_W), BF16),
            pltpu.VMEM((TQ, CONV_C), BF16),
            pltpu.VMEM((TQ, d), BF16),
            pltpu.VMEM((N_Q, BLOCK, 2 * BLOCK), F32),
            pltpu.VMEM((TQ, d), F32),
            pltpu.VMEM((TQ, d), BF16),
            pltpu.VMEM((d, 2 * CONV_C), BF16),
        ],
        compiler_params=pltpu.CompilerParams(dimension_semantics=("arbitrary",), vmem_limit_bytes=VMEM_LIMIT),
        name="layer",
    )(sinks, x2, g, w_in, b_in, cw, cb, lng, lnb, wap, wcp, bcp, wout, g2, w1, w2, gf)


def kernel(x, mix_norm_g, w_in, b_in, sinks, conv_w, conv_b, conv_ln_g, conv_ln_b, w_attn_proj, w_conv_proj,
           b_conv_proj, w_out, mlp_norm_g, w_mlp1, w_mlp2, final_norm_g):
    b, s, d = x.shape
    depth = w_in.shape[0]
    assert s % TQ == 0 and TQ % BLOCK == 0 and TQ % CONV_ROWS == 0
    row = lambda v: v.reshape(1, -1)
    x2 = x.reshape(b * s, d)
    for l in range(depth):
        x2 = _layer(x2, s // TQ, l == depth - 1, sinks[l], row(mix_norm_g[l]), w_in[l].astype(BF16), row(b_in[l]),
                    conv_w[l], row(conv_b[l]), row(conv_ln_g[l]), row(conv_ln_b[l]), w_attn_proj[l].astype(BF16),
                    w_conv_proj[l].astype(BF16), row(b_conv_proj[l]), w_out[l].astype(BF16), row(mlp_norm_g[l]),
                    w_mlp1[l].astype(BF16), w_mlp2[l].astype(BF16), row(final_norm_g))
    return x2.reshape(b, s, d)
```

```python
import functools
import math

import jax
import jax.numpy as jnp
from jax import lax
from jax.experimental import pallas as pl
from jax.experimental.pallas import tpu as pltpu

D_MODEL = 1024
N_Q = 8
N_KV = 2
GROUP = N_Q // N_KV
HEAD_DIM = 64
ATTN_W = N_Q * HEAD_DIM
KV_W = N_KV * HEAD_DIM
WINDOW = 128
BLOCK = 128
CONV_C = D_MODEL // 2
CONV_K = 31
D_FF = 4 * D_MODEL
IN_W = ATTN_W + 2 * KV_W + 2 * CONV_C + 2 * D_MODEL
EPS = 1e-6
NEG = -1e30
SCALE = 1.0 / math.sqrt(HEAD_DIM)
SLOPES = tuple(2.0 ** (-8.0 * h / N_Q) for h in range(1, N_Q + 1))

Q0 = 0
K0 = Q0 + ATTN_W
GA0 = K0 + 2 * KV_W
GB0 = GA0 + CONV_C
GTA0 = GB0 + CONV_C
GTC0 = GTA0 + D_MODEL

LANES = 128
GW = GROUP * HEAD_DIM
N_SLAB = CONV_C // LANES
UPAD = 32
TQ = 512
CONV_ROWS = 64
TAP_GROUP = 4
FF_CHUNK = 512
VMEM_LIMIT = 58 * 1024 * 1024

F32 = jnp.float32
BF16 = jnp.bfloat16


def _dot(a, b):
    return jnp.dot(a, b, preferred_element_type=F32)


def _sigmoid(x):
    return 0.5 * jnp.tanh(0.5 * x) + 0.5


def _rms_norm(x, g):
    ms = jnp.mean(x * x, axis=-1, keepdims=True)
    return x * lax.rsqrt(ms + EPS) * g


def _zero_after(v):
    rows = 32 // jnp.dtype(v.dtype).itemsize
    bits = pltpu.bitcast(v[-rows:, -LANES:], jnp.uint32)
    half = jnp.uint32(16)
    return lax.shift_right_logical(lax.shift_right_logical(bits, half), half).astype(F32)


def _layer_kernel(sinks_ref, x_ref, g_ref, win_ref, bin_ref, cw_ref, cb_ref, lng_ref, lnb_ref,
                  wap_ref, wcp_ref, bcp_ref, wout_ref, g2_ref, w1_ref, w2_ref, gf_ref, o_ref,
                  h_s, q_s, kt_s, vt_s, u_s, y_s, attn_s, cv_s, mg_s, alibi_s, xm_s, h2_s, wglu_s,
                  *, tiles_per_seq, final_norm):
    i = pl.program_id(0)
    j = lax.rem(i, tiles_per_seq)
    tq = x_ref.shape[0]

    qi = lax.broadcasted_iota(jnp.int32, (BLOCK, 2 * BLOCK), 0)
    kj = lax.broadcasted_iota(jnp.int32, (BLOCK, 2 * BLOCK), 1)
    dist = qi + BLOCK - kj
    in_window = (dist >= 0) & (dist < WINDOW)

    @pl.when(i == 0)
    def _():
        xm_s[...] = jnp.zeros(xm_s.shape, F32)
        h2_s[...] = jnp.zeros(h2_s.shape, BF16)
        distf = dist.astype(F32)
        for head in range(N_Q):
            alibi_s[head] = SLOPES[head] * distf
        for c in range(N_SLAB):
            wglu_s[:, 2 * c * LANES:(2 * c + 1) * LANES] = win_ref[:, GA0 + c * LANES:GA0 + (c + 1) * LANES]
            wglu_s[:, (2 * c + 1) * LANES:(2 * c + 2) * LANES] = win_ref[:, GB0 + c * LANES:GB0 + (c + 1) * LANES]

    @pl.when(j == 0)
    def _():
        kt_s[:, 0:BLOCK, :] = jnp.zeros((N_KV, BLOCK, GW), BF16)
        vt_s[:, 0:BLOCK, :] = jnp.zeros((N_KV, BLOCK, GW), BF16)
        u_s[:, 0:UPAD, :] = jnp.zeros((N_SLAB, UPAD, LANES), F32)

    def proj(c0, width):
        return _dot(h_s[...], win_ref[:, c0:c0 + width]) + bin_ref[:, c0:c0 + width]

    def mixer_norm():
        h_s[...] = _rms_norm(x_ref[...], g_ref[...]).astype(BF16)

    def mixer_glu():
        res = _dot(h_s[...], wglu_s[...])
        for c in range(N_SLAB):
            val = res[:, 2 * c * LANES:(2 * c + 1) * LANES] + bin_ref[:, GA0 + c * LANES:GA0 + (c + 1) * LANES]
            gate = res[:, (2 * c + 1) * LANES:(2 * c + 2) * LANES] + bin_ref[:, GB0 + c * LANES:GB0 + (c + 1) * LANES]
            u_s[c, UPAD:UPAD + tq, :] = val * _sigmoid(gate)

    def mixer_conv(r, after):
        n_vreg = CONV_ROWS // 16
        for c in range(N_SLAB):
            lanes = slice(c * LANES, (c + 1) * LANES)
            base = r * CONV_ROWS + UPAD - (CONV_K - 1)
            gate = after
            even = odd = cb_ref[:, lanes] + jnp.tile(after, (n_vreg, 1))
            w_prev = None
            for t in range(CONV_K + 1):
                rows = u_s[c, pl.ds(base + t, CONV_ROWS // 2, stride=2), :]
                if t > 0:
                    odd = odd + w_prev * rows
                if t < CONV_K:
                    if t > 0 and t % TAP_GROUP == 0:
                        gate = _zero_after(even)
                    w_prev = jnp.tile(cw_ref[t:t + 1, lanes] + gate, (n_vreg, 1))
                    even = even + w_prev * rows
            y_s[c, pl.ds(r * CONV_ROWS, CONV_ROWS // 2, stride=2), :] = even
            y_s[c, pl.ds(r * CONV_ROWS + 1, CONV_ROWS // 2, stride=2), :] = odd

    def mixer_conv_act():
        y = jnp.concatenate([y_s[c] for c in range(N_SLAB)], axis=1)
        mu = jnp.mean(y, axis=-1, keepdims=True)
        yc = y - mu
        var = jnp.mean(yc * yc, axis=-1, keepdims=True)
        z = yc * lax.rsqrt(var + EPS) * lng_ref[...] + lnb_ref[...]
        cv_s[...] = (z * _sigmoid(z)).astype(BF16)

    def mixer_qkv():
        qkv = proj(Q0, ATTN_W + 2 * KV_W)
        q_s[...] = qkv[:, :ATTN_W].astype(BF16)
        kv = qkv[:, ATTN_W:]
        lo = lax.broadcasted_iota(jnp.int32, (tq, LANES), 1) < HEAD_DIM
        for src, dst in ((kv[:, :KV_W] * SCALE, kt_s), (kv[:, KV_W:], vt_s)):
            rolled = pltpu.roll(src, HEAD_DIM, axis=1)
            g0 = jnp.where(lo, src, rolled).astype(BF16)
            g1 = jnp.where(lo, rolled, src).astype(BF16)
            dst[0, BLOCK:BLOCK + tq, :] = jnp.concatenate([g0, g0], axis=1)
            dst[1, BLOCK:BLOCK + tq, :] = jnp.concatenate([g1, g1], axis=1)

    head_of_lane = lax.broadcasted_iota(jnp.int32, (BLOCK, GW), 1) // HEAD_DIM
    kmin_first = jnp.where(j == 0, BLOCK, 0)

    def mixer_scores(qb, g):
        valid = in_window & (kj >= kmin_first) if qb == 0 else in_window
        rows = slice(qb * BLOCK, (qb + 1) * BLOCK)
        q4 = q_s[rows, g * GW:(g + 1) * GW]
        lhs = jnp.concatenate(
            [jnp.where(head_of_lane == hh, q4, jnp.zeros_like(q4)) for hh in range(GROUP)], axis=0)
        keys = kt_s[g, qb * BLOCK:qb * BLOCK + 2 * BLOCK, :]
        s = lax.dot_general(lhs, keys, (((1,), (1,)), ((), ())), preferred_element_type=F32)
        ps, inv_ls = [], []
        for hh in range(GROUP):
            head = g * GROUP + hh
            sink = sinks_ref[head]
            sh = jnp.where(valid, s[hh * BLOCK:(hh + 1) * BLOCK, :] - alibi_s[head], NEG)
            m = jnp.maximum(jnp.max(sh, axis=-1, keepdims=True), sink)
            p = jnp.exp(sh - m)
            denom = jnp.sum(p, axis=-1, keepdims=True) + jnp.exp(sink - m)
            ps.append(p.astype(BF16))
            inv_ls.append(1.0 / denom)
        return qb, g, jnp.concatenate(ps, axis=0), inv_ls

    def mixer_attend(qb, g, pmat, inv_ls):
        rows = slice(qb * BLOCK, (qb + 1) * BLOCK)
        o = _dot(pmat, vt_s[g, qb * BLOCK:qb * BLOCK + 2 * BLOCK, :])
        out = jnp.zeros((BLOCK, GW), F32)
        for hh in range(GROUP):
            out = jnp.where(head_of_lane == hh, o[hh * BLOCK:(hh + 1) * BLOCK, :] * inv_ls[hh], out)
        attn_s[rows, g * GW:(g + 1) * GW] = out.astype(BF16)

    def mixer_merge(nc):
        half = D_MODEL // 2
        cols = slice(nc * half, (nc + 1) * half)
        gate_a = _sigmoid(proj(GTA0 + nc * half, half))
        gate_c = _sigmoid(proj(GTC0 + nc * half, half))
        br_c = _dot(cv_s[...], wcp_ref[:, cols]) + bcp_ref[:, cols]
        br_a = _dot(attn_s[...], wap_ref[:, cols])
        mg_s[:, cols] = (gate_a * br_a + gate_c * br_c).astype(BF16)

    def mixer_out():
        xm_s[...] = x_ref[...] + _dot(mg_s[...], wout_ref[...])
        kt_s[:, 0:BLOCK, :] = kt_s[:, tq:tq + BLOCK, :]
        vt_s[:, 0:BLOCK, :] = vt_s[:, tq:tq + BLOCK, :]
        u_s[:, 0:UPAD, :] = u_s[:, tq:tq + UPAD, :]

    def mlp_norm():
        h2_s[...] = _rms_norm(xm_s[...], g2_ref[...]).astype(BF16)

    def mlp_up(c):
        cols = slice(c * FF_CHUNK, (c + 1) * FF_CHUNK)
        a = jnp.maximum(_dot(h2_s[...], w1_ref[:, cols]), 0.0)
        return (a * a).astype(BF16)

    def mlp_down(c, a):
        part = _dot(a, w2_ref[c * FF_CHUNK:(c + 1) * FF_CHUNK, :])
        if c == 0:
            o_ref[...] = xm_s[...] + part
        else:
            o_ref[...] += part
        return _zero_after(part)

    n_ff = D_FF // FF_CHUNK
    mixer_norm()
    after = mlp_down(0, mlp_up(0))
    mixer_glu()
    mixer_qkv()
    conv_chunk = 0
    for c in range(1, 1 + tq // CONV_ROWS // 2):
        a = mlp_up(c)
        mixer_conv(conv_chunk, after)
        after = _zero_after(a)
        nxt = mlp_down(c, a)
        mixer_conv(conv_chunk + 1, after)
        after = nxt
        conv_chunk += 2
    mixer_conv_act()
    blocks = [(qb, g) for qb in range(tq // BLOCK) for g in range(N_KV)]
    pending = [mixer_scores(*blk) for blk in blocks[0:2]]
    mlp_chunks = []
    for c in range(1 + tq // CONV_ROWS // 2, n_ff - 1):
        mlp_chunks += [("up", c), ("down", c)]
    mlp_chunks.append(("up", n_ff - 1))
    a = None
    for k, (kind, c) in enumerate(mlp_chunks):
        if kind == "up":
            a = mlp_up(c)
        else:
            mlp_down(c, a)
        for scored in pending:
            mixer_attend(*scored)
        pending = [mixer_scores(*blk) for blk in blocks[2 * k + 2:2 * k + 4]]
    a_last = a
    mixer_merge(0)
    mixer_merge(1)
    mixer_out()
    mlp_down(n_ff - 1, a_last)
    mlp_norm()
    if final_norm:
        o_ref[...] = _rms_norm(o_ref[...], gf_ref[...])


def _resident(shape):
    return pl.BlockSpec(shape, lambda *_: (0,) * len(shape), pipeline_mode=pl.Buffered(1))


def _resident_layer(shape, layer):
    return pl.BlockSpec((None,) + shape, lambda *_: (layer,) + (0,) * len(shape), pipeline_mode=pl.Buffered(1))


def _layer(x2, tiles_per_seq, final_norm, layer, sinks, g, w_in, b_in, cw, cb, lng, lnb, wap, wcp, bcp, wout, g2, w1, w2, gf):
    t, d = x2.shape
    n_tiles = t // TQ
    assert TQ // CONV_ROWS == 8 and D_FF // FF_CHUNK == 8 and (TQ // BLOCK) * N_KV == 8
    return pl.pallas_call(
        functools.partial(_layer_kernel, tiles_per_seq=tiles_per_seq, final_norm=final_norm),
        grid=(n_tiles + 1,),
        in_specs=[
            pl.BlockSpec(memory_space=pltpu.SMEM),
            pl.BlockSpec((TQ, d), lambda i: (jnp.minimum(i, n_tiles - 1), 0)),
            _resident((1, d)),
            _resident_layer((d, IN_W), layer),
            _resident((1, IN_W)),
            _resident((CONV_K, CONV_C)),
            _resident((1, CONV_C)),
            _resident((1, CONV_C)),
            _resident((1, CONV_C)),
            _resident_layer((ATTN_W, d), layer),
            _resident_layer((CONV_C, d), layer),
            _resident((1, d)),
            _resident_layer((d, d), layer),
            _resident((1, d)),
            _resident_layer((d, D_FF), layer),
            _resident_layer((D_FF, d), layer),
            _resident((1, d)),
        ],
        out_specs=pl.BlockSpec((TQ, d), lambda i: (jnp.maximum(i - 1, 0), 0)),
        out_shape=jax.ShapeDtypeStruct(x2.shape, x2.dtype),
        scratch_shapes=[
            pltpu.VMEM((TQ, d), BF16),
            pltpu.VMEM((TQ, ATTN_W), BF16),
            pltpu.VMEM((N_KV, BLOCK + TQ, GW), BF16),
            pltpu.VMEM((N_KV, BLOCK + TQ, GW), BF16),
            pltpu.VMEM((N_SLAB, UPAD + TQ, LANES), F32),
            pltpu.VMEM((N_SLAB, TQ, LANES), F32),
            pltpu.VMEM((TQ, ATTN_W), BF16),
            pltpu.VMEM((TQ, CONV_C), BF16),
            pltpu.VMEM((TQ, d), BF16),
            pltpu.VMEM((N_Q, BLOCK, 2 * BLOCK), F32),
            pltpu.VMEM((TQ, d), F32),
            pltpu.VMEM((TQ, d), BF16),
            pltpu.VMEM((d, 2 * CONV_C), BF16),
        ],
        compiler_params=pltpu.CompilerParams(dimension_semantics=("arbitrary",), vmem_limit_bytes=VMEM_LIMIT),
        name="layer",
    )(sinks, x2, g, w_in, b_in, cw, cb, lng, lnb, wap, wcp, bcp, wout, g2, w1, w2, gf)


def kernel(x, mix_norm_g, w_in, b_in, sinks, conv_w, conv_b, conv_ln_g, conv_ln_b, w_attn_proj, w_conv_proj,
           b_conv_proj, w_out, mlp_norm_g, w_mlp1, w_mlp2, final_norm_g):
    b, s, d = x.shape
    depth = w_in.shape[0]
    assert s % TQ == 0 and TQ % BLOCK == 0 and TQ % CONV_ROWS == 0
    row = lambda v: v.reshape(1, -1)
    x2 = x.reshape(b * s, d)
    w_in, w_attn_proj, w_conv_proj, w_out, w_mlp1, w_mlp2 = (
        w.astype(BF16) for w in (w_in, w_attn_proj, w_conv_proj, w_out, w_mlp1, w_mlp2))
    for l in range(depth):
        x2 = _layer(x2, s // TQ, l == depth - 1, l, sinks[l], row(mix_norm_g[l]), w_in, row(b_in[l]),
                    conv_w[l], row(conv_b[l]), row(conv_ln_g[l]), row(conv_ln_b[l]), w_attn_proj,
                    w_conv_proj, row(b_conv_proj[l]), w_out, row(mlp_norm_g[l]),
                    w_mlp1, w_mlp2, row(final_norm_g))
    return x2.reshape(b, s, d)
```
